```python
import functools
import jax, jax.numpy as jnp
from jax import lax
import numpy as np

D_MODEL = 1024
BATCH = 8
SEQ = 2048
DEPTH = 2
DEC_BATCH = 128
DEC_SEQ = 4
PAST_LEN = 2048
PAGE_SIZE = 128

N_BRANCH = 4
POOL_GROUPS = 4
POOL_GROUP_W = 64
POOL_W = POOL_GROUPS * POOL_GROUP_W
POOL_WINDOWS = (2, 4, 8, 16)
POOL_STATE = 15
CONF_W = 256
CONF_K = 31
SCONV_W = 256
SCONV_K = 3
N_HEADS = 8
HEAD_DIM = 64
ATT_W = N_HEADS * HEAD_DIM
IDX_HEADS = 4
IDX_DIM = 64
TOPK_MAX = 256
Q_BLOCK = 128
ROPE_THETA = 10000.0
PE_DIM = 256
D_FF = ((8 * D_MODEL + 3 * 256 - 1) // (3 * 256)) * 256
ALPHA = (2 * DEPTH) ** 0.25
BETA = (8 * DEPTH) ** -0.25
LN_EPS = 1e-5
IN_SPLITS = (N_BRANCH * D_MODEL, POOL_W, 2 * CONF_W, 3 * SCONV_W, ATT_W, ATT_W, ATT_W,
             IDX_HEADS * IDX_DIM, IDX_DIM, IDX_HEADS)
IN_COLS = sum(IN_SPLITS)

kernel_name = 'hybrid_pool_conformer_shortconv_dsa_step'


def layer_norm(x, g, b):
    xf = x.astype(jnp.float32)
    mu = jnp.mean(xf, -1, keepdims=True)
    var = jnp.mean(jnp.square(xf - mu), -1, keepdims=True)
    return ((xf - mu) * lax.rsqrt(var + LN_EPS) * g + b).astype(x.dtype)


def split_cols(z):
    pts = np.cumsum(IN_SPLITS)[:-1].tolist()
    return jnp.split(z, pts, axis=-1)


def rope(x, pos):
    half = x.shape[-1] // 2
    inv = ROPE_THETA ** (-jnp.arange(half, dtype=jnp.float32) / half)
    ang = pos.astype(jnp.float32)[:, None] * inv[None, :]
    cos = jnp.cos(ang)[:, None, :]
    sin = jnp.sin(ang)[:, None, :]
    xf = x.astype(jnp.float32)
    x1, x2 = xf[..., :half], xf[..., half:]
    return jnp.concatenate([x1 * cos - x2 * sin, x2 * cos + x1 * sin], -1).astype(x.dtype)


def causal_dwconv(u, prefix, w):
    ext = jnp.concatenate([prefix.astype(u.dtype), u], axis=1)
    y = lax.conv_general_dilated(ext, w[:, None, :].astype(u.dtype), window_strides=(1,), padding='VALID',
                                 dimension_numbers=('NWC', 'WIO', 'NWC'), feature_group_count=u.shape[-1])
    return y, ext[:, ext.shape[1] - (w.shape[0] - 1):]


def pool_mixer(u, prefix, pos, mix_w, scale):
    n, t, _ = u.shape
    p = prefix.shape[1]
    ext = jnp.concatenate([prefix.astype(u.dtype), u], axis=1)
    extf = ext.astype(jnp.float32)
    cs = jnp.concatenate([jnp.zeros((n, 1, POOL_W), jnp.float32), jnp.cumsum(extf, axis=1)], axis=1)
    end = cs[:, p + 1:p + 1 + t]
    means = []
    for g, win in enumerate(POOL_WINDOWS):
        sl = slice(g * POOL_GROUP_W, (g + 1) * POOL_GROUP_W)
        cnt = jnp.minimum(win, pos + 1).astype(jnp.float32)[None, :, None]
        means.append((end[..., sl] - cs[:, p + 1 - win:p + 1 - win + t, sl]) / cnt)
    pooled = (jnp.concatenate(means, -1) - extf[:, p:]).astype(u.dtype)
    mixed = jnp.einsum('ntgc,gcd->ntgd', pooled.reshape(n, t, POOL_GROUPS, POOL_GROUP_W), mix_w)
    return mixed.reshape(n, t, POOL_W) * scale, ext[:, ext.shape[1] - POOL_STATE:]


def index_scores(qi, ki, wi):
    s = jnp.einsum('nqhd,nkd->nqhk', qi.astype(jnp.float32), ki.astype(jnp.float32)) * IDX_DIM ** -0.5
    return jnp.einsum('nqhk,nqh->nqk', jax.nn.relu(s), wi.astype(jnp.float32)) * IDX_HEADS ** -0.5


def sparse_attend(q, kg, vg, valid):
    logits = jnp.einsum('nqhd,nqkhd->nqhk', q, kg).astype(jnp.float32) * HEAD_DIM ** -0.5
    logits = jnp.where(valid[:, :, None, :], logits, -jnp.inf)
    probs = jax.nn.softmax(logits, axis=-1)
    return jnp.einsum('nqhk,nqkhd->nqhd', probs.astype(vg.dtype), vg)


def dsa_prompt(q, k, v, qi, ki, wi):
    n, s, h, d = q.shape
    nb = s // Q_BLOCK
    ktop = min(TOPK_MAX, s // 4)
    key_pos = jnp.arange(s)

    def to_blocks(a):
        return a.reshape((n, nb, Q_BLOCK) + a.shape[2:]).swapaxes(0, 1)

    def block(args):
        qb, qib, wib, posb = args
        sc = index_scores(qib, ki, wib)
        sc = jnp.where(key_pos[None, None, :] <= posb[None, :, None], sc, -jnp.inf)
        vals, idx = lax.top_k(sc, ktop)
        kg = jax.vmap(lambda a, i: a[i])(k, idx)
        vg = jax.vmap(lambda a, i: a[i])(v, idx)
        return sparse_attend(qb, kg, vg, jnp.isfinite(vals))

    out = lax.map(block, (to_blocks(q), to_blocks(qi), to_blocks(wi), key_pos.reshape(nb, Q_BLOCK)))
    return out.swapaxes(0, 1).reshape(n, s, h, d)


def dsa_sample(q, k, v, qi, ki, wi, layer, cache_k, cache_v, cache_kidx, page_table):
    n, t, h, d = q.shape
    past = page_table.shape[1] * PAGE_SIZE
    n_phys = cache_k.shape[1]
    ktop = min(TOPK_MAX, (past + t) // 4)
    row_ids = layer * n_phys * PAGE_SIZE + (page_table[:, :, None] * PAGE_SIZE
                                            + jnp.arange(PAGE_SIZE)[None, None, :]).reshape(n, past)
    ki_all = jnp.concatenate([cache_kidx.reshape(-1, IDX_DIM)[row_ids].astype(ki.dtype), ki], axis=1)
    sc = index_scores(qi, ki_all, wi)
    qpos = past + jnp.arange(t)
    sc = jnp.where(jnp.arange(past + t)[None, None, :] <= qpos[None, :, None], sc, -jnp.inf)
    vals, idx = lax.top_k(sc, ktop)
    from_past = (idx < past)[..., None, None]
    phys = jax.vmap(lambda r, i: r[i])(row_ids, jnp.clip(idx, 0, past - 1))
    new_i = jnp.clip(idx - past, 0, t - 1)
    kg = jnp.where(from_past, cache_k.reshape(-1, h, d)[phys].astype(k.dtype), jax.vmap(lambda a, i: a[i])(k, new_i))
    vg = jnp.where(from_past, cache_v.reshape(-1, h, d)[phys].astype(v.dtype), jax.vmap(lambda a, i: a[i])(v, new_i))
    return sparse_attend(q, kg, vg, jnp.isfinite(vals))


def trunk_layer(x, pe, pos0, st_pool, st_conf, st_sc, attend,
                w_in, pool_mix, pool_scale, conv_w, conv_b, conv_ln_g, conv_ln_b, sconv_w,
                w_br_pool, w_br_conv, w_br_sconv, w_br_attn, w_o, ln1_g, ln1_b,
                w_ffn_in, w_ffn_out, w_pe, w_pg, ln2_g, ln2_b):
    n, t, _ = x.shape
    pos = pos0 + jnp.arange(t)
    g_cols, u_pool, u_conf, u_sc, q, k, v, qi, ki, wi = split_cols(x @ w_in)
    a_out, new_pool = pool_mixer(u_pool, st_pool, pos, pool_mix, pool_scale)
    ga, gb = jnp.split(u_conf, 2, axis=-1)
    cv, new_conf = causal_dwconv(ga * jax.nn.sigmoid(gb), st_conf, conv_w)
    b_out = jax.nn.silu(layer_norm(cv + conv_b, conv_ln_g, conv_ln_b))
    sb, sc_, sh = jnp.split(u_sc, 3, axis=-1)
    cc, new_sc = causal_dwconv(sc_ * sh, st_sc, sconv_w)
    c_out = sb * cc
    q = rope(q.reshape(n, t, N_HEADS, HEAD_DIM), pos)
    k = rope(k.reshape(n, t, N_HEADS, HEAD_DIM), pos)
    v = v.reshape(n, t, N_HEADS, HEAD_DIM)
    qi = rope(qi.reshape(n, t, IDX_HEADS, IDX_DIM), pos)
    ki = rope(ki[:, :, None, :], pos)[:, :, 0]
    d_out = attend(q, k, v, qi, ki, wi).reshape(n, t, ATT_W)
    gates = jax.nn.sigmoid(g_cols.reshape(n, t, N_BRANCH, D_MODEL))
    merged = (gates[:, :, 0] * (a_out @ w_br_pool) + gates[:, :, 1] * (b_out @ w_br_conv)
              + gates[:, :, 2] * (c_out @ w_br_sconv) + gates[:, :, 3] * (d_out @ w_br_attn))
    x1 = layer_norm(ALPHA * x + merged @ w_o, ln1_g, ln1_b)
    hg, hu = jnp.split(x1 @ w_ffn_in, 2, axis=-1)
    r = ALPHA * x1 + (jax.nn.silu(hg) * hu) @ w_ffn_out
    r = r + (pe @ w_pe) * jax.nn.sigmoid(r @ w_pg)
    y = layer_norm(r, ln2_g, ln2_b)
    return y, (k, v, ki), (new_pool, new_conf, new_sc)


def setup_inputs(seed: int = 0) -> dict:
    key = jax.random.key(seed)
    ks = jax.random.split(key, 40)
    f32 = jnp.float32

    def nrm(i, shape, scale=1.0):
        return jax.random.normal(ks[i], shape, f32) * scale

    n_pages = PAST_LEN // PAGE_SIZE
    n_used = DEC_BATCH * n_pages
    n_phys = n_used + n_used // 4
    page_table = jax.random.permutation(ks[8], n_phys)[:n_used].reshape(DEC_BATCH, n_pages).astype(jnp.int32)
    return {
        'x_prompt': nrm(0, (BATCH, SEQ, D_MODEL)),
        'x_sample': nrm(1, (DEC_BATCH, DEC_SEQ, D_MODEL)),
        'cache_k': nrm(2, (DEPTH, n_phys, PAGE_SIZE, N_HEADS, HEAD_DIM)),
        'cache_v': nrm(3, (DEPTH, n_phys, PAGE_SIZE, N_HEADS, HEAD_DIM)),
        'cache_kidx': nrm(4, (DEPTH, n_phys, PAGE_SIZE, IDX_DIM)),
        'state_pool': nrm(5, (DEPTH, DEC_BATCH, POOL_STATE, POOL_W)),
        'state_conv': nrm(6, (DEPTH, DEC_BATCH, CONF_K - 1, CONF_W)),
        'state_sconv': nrm(7, (DEPTH, DEC_BATCH, SCONV_K - 1, SCONV_W)),
        'page_table': page_table,
        'p_prompt': nrm(9, (DEPTH, BATCH, SEQ, PE_DIM)),
        'p_sample': nrm(10, (DEPTH, DEC_BATCH, DEC_SEQ, PE_DIM)),
        'w_in': nrm(11, (DEPTH, D_MODEL, IN_COLS), D_MODEL ** -0.5),
        'pool_mix': nrm(12, (DEPTH, POOL_GROUPS, POOL_GROUP_W, POOL_GROUP_W), POOL_GROUP_W ** -0.5),
        'pool_scale': 1.0 + nrm(13, (DEPTH, POOL_W), 0.02),
        'conv_w': nrm(14, (DEPTH, CONF_K, CONF_W), CONF_K ** -0.5),
        'conv_b': nrm(15, (DEPTH, CONF_W), 0.02),
        'conv_ln_g': 1.0 + nrm(16, (DEPTH, CONF_W), 0.02),
        'conv_ln_b': nrm(17, (DEPTH, CONF_W), 0.02),
        'sconv_w': nrm(18, (DEPTH, SCONV_K, SCONV_W), SCONV_K ** -0.5),
        'w_br_pool': nrm(19, (DEPTH, POOL_W, D_MODEL), POOL_W ** -0.5 * BETA),
        'w_br_conv': nrm(20, (DEPTH, CONF_W, D_MODEL), CONF_W ** -0.5 * BETA),
        'w_br_sconv': nrm(21, (DEPTH, SCONV_W, D_MODEL), SCONV_W ** -0.5 * BETA),
        'w_br_attn': nrm(22, (DEPTH, ATT_W, D_MODEL), ATT_W ** -0.5 * BETA),
        'w_o': nrm(23, (DEPTH, D_MODEL, D_MODEL), D_MODEL ** -0.5 * BETA),
        'ln1_g': 1.0 + nrm(24, (DEPTH, D_MODEL), 0.02),
        'ln1_b': nrm(25, (DEPTH, D_MODEL), 0.02),
        'w_ffn_in': nrm(26, (DEPTH, D_MODEL, 2 * D_FF), D_MODEL ** -0.5),
        'w_ffn_out': nrm(27, (DEPTH, D_FF, D_MODEL), D_FF ** -0.5 * BETA),
        'w_pe': nrm(28, (DEPTH, PE_DIM, D_MODEL), PE_DIM ** -0.5 * BETA),
        'w_pg': nrm(29, (DEPTH, D_MODEL, D_MODEL), D_MODEL ** -0.5),
        'ln2_g': 1.0 + nrm(30, (DEPTH, D_MODEL), 0.02),
        'ln2_b': nrm(31, (DEPTH, D_MODEL), 0.02),
    }


def reference(x_prompt, x_sample, cache_k, cache_v, cache_kidx, state_pool, state_conv, state_sconv,
              page_table, p_prompt, p_sample,
              w_in, pool_mix, pool_scale, conv_w, conv_b, conv_ln_g, conv_ln_b, sconv_w,
              w_br_pool, w_br_conv, w_br_sconv, w_br_attn, w_o, ln1_g, ln1_b,
              w_ffn_in, w_ffn_out, w_pe, w_pg, ln2_g, ln2_b):
    past = page_table.shape[1] * PAGE_SIZE
    n_b = x_prompt.shape[0]
    dt = x_prompt.dtype
    zero_pool = jnp.zeros((n_b, POOL_STATE, POOL_W), dt)
    zero_conf = jnp.zeros((n_b, CONF_K - 1, CONF_W), dt)
    zero_sc = jnp.zeros((n_b, SCONV_K - 1, SCONV_W), dt)
    y_p, y_s = x_prompt, x_sample
    kp, vp, kip, ks_, vs, kis = [], [], [], [], [], []
    pool_p, pool_s, conf_p, conf_s, sc_p, sc_s = [], [], [], [], [], []
    for i in range(DEPTH):
        lw = (w_in[i], pool_mix[i], pool_scale[i], conv_w[i], conv_b[i], conv_ln_g[i], conv_ln_b[i], sconv_w[i],
              w_br_pool[i], w_br_conv[i], w_br_sconv[i], w_br_attn[i], w_o[i], ln1_g[i], ln1_b[i],
              w_ffn_in[i], w_ffn_out[i], w_pe[i], w_pg[i], ln2_g[i], ln2_b[i])
        y_p, kv_p, st_p = trunk_layer(y_p, p_prompt[i], 0, zero_pool, zero_conf, zero_sc, dsa_prompt, *lw)
        attend_s = functools.partial(dsa_sample, layer=i, cache_k=cache_k, cache_v=cache_v,
                                     cache_kidx=cache_kidx, page_table=page_table)
        y_s, kv_s, st_s = trunk_layer(y_s, p_sample[i], past, state_pool[i], state_conv[i], state_sconv[i],
                                      attend_s, *lw)
        kp.append(kv_p[0]); vp.append(kv_p[1]); kip.append(kv_p[2])
        ks_.append(kv_s[0]); vs.append(kv_s[1]); kis.append(kv_s[2])
        pool_p.append(st_p[0]); conf_p.append(st_p[1]); sc_p.append(st_p[2])
        pool_s.append(st_s[0]); conf_s.append(st_s[1]); sc_s.append(st_s[2])
    return (y_p, y_s,
            jnp.stack(kp), jnp.stack(vp), jnp.stack(kip),
            jnp.stack(ks_), jnp.stack(vs), jnp.stack(kis),
            jnp.stack(pool_p), jnp.stack(pool_s),
            jnp.stack(conf_p), jnp.stack(conf_s),
            jnp.stack(sc_p), jnp.stack(sc_s))
```

```python
import functools

import numpy as np
import jax
import jax.numpy as jnp
from jax import lax
from jax.experimental import pallas as pl
from jax.experimental.pallas import tpu as pltpu

F32, BF16, I32 = jnp.float32, jnp.bfloat16, jnp.int32

N_BRANCH = 4
POOL_GROUP_W = 64
POOL_W = 256
POOL_WINDOWS = (2, 4, 8, 16)
POOL_STATE = 15
CONF_W = 256
CONF_K = 31
SCONV_W = 256
SCONV_K = 3
N_HEADS = 8
HEAD_DIM = 64
ATT_W = N_HEADS * HEAD_DIM
IDX_HEADS = 4
IDX_DIM = 64
TOPK_MAX = 256
ROPE_THETA = 10000.0
PAGE_SIZE = 128
LN_EPS = 1e-5

LANES = 128
SUBLANES = 8
MXU_DEPTH = 256
VMEM_LIMIT = 56 * 1024 * 1024
FFN_SUB = 768
FFN_VMEM_LIMIT = 60 * 1024 * 1024

COL_UP, COL_GA, COL_GB, COL_SB, COL_SC, COL_SH = 0, 256, 512, 768, 1024, 1280
COL_Q, COL_K, COL_V, COL_QI, COL_KW = 1536, 2048, 2560, 3072, 3328
IN_COLS = 7492
NZ = 3584
WI_LANE = IDX_DIM

INT_MIN = -2 ** 31
NEG_INF_KEY = -2139095041
POS_INF_KEY = 0x7F800000
MASKED = -1e30


def _pick(m, prefs):
    for p in prefs:
        if m % p == 0:
            return p
    return m


def _dot(a, b):
    return jnp.dot(a, b, preferred_element_type=F32)


def _dot_nt(a, b):
    return lax.dot_general(a, b, (((1,), (1,)), ((), ())), preferred_element_type=F32)


def _sigmoid(x):
    return jax.nn.sigmoid(x)


def _layer_norm(x, g, b):
    mu = jnp.mean(x, axis=-1, keepdims=True)
    xc = x - mu
    var = jnp.mean(xc * xc, axis=-1, keepdims=True)
    return xc * lax.rsqrt(var + LN_EPS) * g + b


IDX_PARTS = 2
IDX_TERMS = ((0, 0), (1, 0), (0, 1))
IDX_K = MXU_DEPTH


def _split_parts(x):
    parts, rest = [], x
    for i in range(IDX_PARTS):
        part = rest.astype(BF16)
        parts.append(part)
        if i + 1 < IDX_PARTS:
            rest = rest - part.astype(F32)
    return tuple(parts)


def _lane_tile(t, width):
    reps = width // t.shape[-1]
    return t if reps == 1 else jnp.concatenate([t] * reps, axis=-1)


def _rope(x, cos, sin_signed):
    axis = x.ndim - 1
    width = x.shape[-1]
    lane = lax.broadcasted_iota(I32, x.shape, axis)
    first = (lane & (HEAD_DIM // 2)) == 0
    swapped = jnp.where(first, pltpu.roll(x, width - HEAD_DIM // 2, axis), pltpu.roll(x, HEAD_DIM // 2, axis))
    return x * _lane_tile(cos, width) + swapped * _lane_tile(sin_signed, width)


def _order_key(score):
    bits = pltpu.bitcast(score, I32)
    return jnp.where(bits < 0, bits ^ 0x7FFFFFFF, bits)


def _kth_largest_key(key, ktop, digit_bits=2):
    t = jnp.full((key.shape[0], 1), INT_MIN, I32)
    for shift in range(32 - digit_bits, -1, -digit_bits):
        digit = jnp.zeros_like(t)
        for k in range(1, 2 ** digit_bits):
            step = ((k << shift) + 2 ** 31) % 2 ** 32 - 2 ** 31
            cnt = jnp.sum(jnp.where(key >= t + step, 1.0, 0.0), axis=1, keepdims=True)
            digit = digit + jnp.where(cnt >= ktop, 1, 0)
        t = t + digit * (1 << shift)
    return t


def _selection_bias(key, thr, ktop, u_ref, store):
    n_chunks = key.shape[1] // LANES
    gt_cnt = jnp.sum(jnp.where(key > thr, 1.0, 0.0), axis=1, keepdims=True)
    need = ktop - gt_cnt
    carry = jnp.zeros_like(gt_cnt)
    for c in range(n_chunks):
        kc = key[:, c * LANES:(c + 1) * LANES]
        eq = kc == thr
        pre = _dot(jnp.where(eq, 1.0, 0.0).astype(BF16), u_ref[...]) + carry
        carry = pre[:, LANES - 1:LANES]
        sel = (kc > thr) | (eq & (pre <= need))
        ok = sel & (kc > NEG_INF_KEY) & (kc < POS_INF_KEY)
        store(c, jnp.where(ok, 0.0, -jnp.inf))


def _proj_kernel(x_ref, w_ref, o_ref):
    o_ref[...] = _dot(x_ref[...].astype(BF16), w_ref[...])


def _in_proj(x, w, col0):
    m, k = x.shape
    n = w.shape[1] - col0
    tm = _pick(m, (2048, 512))
    tn = 512
    assert col0 % tn == 0 and n % tn == 0
    return pl.pallas_call(
        _proj_kernel,
        grid=(m // tm, n // tn),
        in_specs=[pl.BlockSpec((tm, k), lambda i, j: (i, 0)),
                  pl.BlockSpec((k, tn), lambda i, j: (0, col0 // tn + j))],
        out_specs=pl.BlockSpec((tm, tn), lambda i, j: (i, j)),
        out_shape=jax.ShapeDtypeStruct((m, n), F32),
        compiler_params=pltpu.CompilerParams(dimension_semantics=("parallel", "arbitrary"),
                                             vmem_limit_bytes=VMEM_LIMIT),
        name="in_proj",
    )(x, w)


def _pool_select(sums, pos, lane):
    g = POOL_GROUP_W
    win = jnp.where(lane < g, 2.0, jnp.where(lane < 2 * g, 4.0, jnp.where(lane < 3 * g, 8.0, 16.0)))
    tot = jnp.where(lane < g, sums[2], jnp.where(lane < 2 * g, sums[4], jnp.where(lane < 3 * g, sums[8], sums[16])))
    return tot / jnp.minimum(win, pos + 1.0)


def _place_parts(parts, sel_ref):
    out = _dot(parts[0], sel_ref[0])
    for i in range(1, len(parts)):
        out = out + _dot(parts[i], sel_ref[i])
    return out


def _index_operands(qi_rot, kw_rot, p_ref, r_ref):
    qi3 = _place_parts(_split_parts(qi_rot), p_ref).astype(BF16)
    ki3 = _place_parts(_split_parts(kw_rot), r_ref).astype(BF16)
    return qi3, ki3


def _mix_prompt_kernel(up_ref, ga_ref, gb_ref, sb_ref, sc_ref, sh_ref, q_ref, k_ref, v_ref, qi_ref, kw_ref,
                       cos_ref, sin_ref, wbd_ref, pscale_ref, cw_ref, cb_ref, cg_ref, cbeta_ref, sw_ref,
                       p_ref, r_ref,
                       a_ref, b_ref, c_ref, qbf_ref, kt_ref, kbf_ref, vt_ref, vtc_ref, qi3_ref, ki3_ref,
                       kwt_ref, npool_ref, nconf_ref, nsc_ref,
                       ep, ec, es, *, tt_rows, rc):
    tt = pl.program_id(1)
    n_tt = pl.num_programs(1)
    PP, PC, PS = 16, 32, 8

    @pl.when(tt == 0)
    def _():
        ep[0:PP, :] = jnp.zeros((PP, POOL_W), F32)
        ec[0:PC, :] = jnp.zeros((PC, CONF_W), F32)
        es[0:PS, :] = jnp.zeros((PS, SCONV_W), F32)

    ep[PP:PP + tt_rows, :] = up_ref[...]
    ec[PC:PC + tt_rows, :] = ga_ref[...] * _sigmoid(gb_ref[...])
    es[PS:PS + tt_rows, :] = sc_ref[...] * sh_ref[...]

    lane = lax.broadcasted_iota(I32, (rc, POOL_W), 1)
    row = lax.broadcasted_iota(I32, (rc, POOL_W), 0)
    for r in range(0, tt_rows, rc):
        cur = ep[pl.ds(PP + r, rc), :]
        acc = cur
        sums = {}
        for j in range(1, 16):
            acc = acc + ep[pl.ds(PP + r - j, rc), :]
            if j + 1 in POOL_WINDOWS:
                sums[j + 1] = acc
        pos = (tt * tt_rows + r + row).astype(F32)
        pooled = _pool_select(sums, pos, lane) - cur
        a_ref[pl.ds(r, rc), :] = _dot(pooled.astype(BF16), wbd_ref[...]) * pscale_ref[...]
        cv = cw_ref[pl.ds(0, 1), :] * ec[pl.ds(PC - (CONF_K - 1) + r, rc), :]
        for j in range(1, CONF_K):
            cv = cv + cw_ref[pl.ds(j, 1), :] * ec[pl.ds(PC - (CONF_K - 1) + r + j, rc), :]
        h = _layer_norm(cv + cb_ref[...], cg_ref[...], cbeta_ref[...])
        b_ref[pl.ds(r, rc), :] = h * _sigmoid(h)
        cc = sw_ref[pl.ds(0, 1), :] * es[pl.ds(PS - (SCONV_K - 1) + r, rc), :]
        for j in range(1, SCONV_K):
            cc = cc + sw_ref[pl.ds(j, 1), :] * es[pl.ds(PS - (SCONV_K - 1) + r + j, rc), :]
        c_ref[pl.ds(r, rc), :] = sb_ref[pl.ds(r, rc), :] * cc

    @pl.when(tt == n_tt - 1)
    def _():
        npool_ref[0] = ep[pl.ds(PP + tt_rows - POOL_STATE, POOL_STATE), :]
        nconf_ref[0] = ec[pl.ds(PC + tt_rows - (CONF_K - 1), CONF_K - 1), :]
        nsc_ref[0] = es[pl.ds(PS + tt_rows - (SCONV_K - 1), SCONV_K - 1), :]

    ep[0:PP, :] = ep[tt_rows:tt_rows + PP, :]
    ec[0:PC, :] = ec[tt_rows:tt_rows + PC, :]
    es[0:PS, :] = es[tt_rows:tt_rows + PS, :]

    cos, sin = cos_ref[...], sin_ref[...]
    qbf_ref[...] = _rope(q_ref[...], cos, sin).astype(BF16)
    kr = _rope(k_ref[...], cos, sin)
    kt_ref[0] = kr.T
    kbf_ref[...] = kr.astype(BF16)
    vt = v_ref[...].T
    vt_ref[0] = vt
    vtc_ref[0, 0] = vt.astype(BF16)
    kw = kw_ref[...]
    lane_kw = lax.broadcasted_iota(I32, kw.shape, 1)
    kwr = jnp.where(lane_kw < IDX_DIM, _rope(kw, cos, sin), kw)
    kwt_ref[0] = kwr.T
    qi3, ki3 = _index_operands(_rope(qi_ref[...], cos, sin), kwr,
                               p_ref, r_ref)
    qi3_ref[...] = qi3
    ki3_ref[...] = ki3


def _mix_prompt(z, n, t, cos, sin, consts, lw):
    m = n * t
    tt_rows = _pick(t, (256,))
    n_tt = t // tt_rows
    rc = 64

    def zspec(width, col):
        return pl.BlockSpec((tt_rows, width), lambda i, j, c=col // width: (i * n_tt + j, c))

    def tspec(width):
        return pl.BlockSpec((tt_rows, width), lambda i, j: (j, 0))

    def cspec(a):
        return pl.BlockSpec(a.shape, lambda i, j: (0,) * a.ndim)

    def ospec(width):
        return pl.BlockSpec((tt_rows, width), lambda i, j: (i * n_tt + j, 0))

    def sspec(rows, width):
        return pl.BlockSpec((1, rows, width), lambda i, j: (i, 0, 0))

    weights = (lw["wbd"], lw["pool_scale"], lw["conv_w"], lw["conv_b"], lw["conv_ln_g"], lw["conv_ln_b"],
               lw["sconv_w"], consts["p"], consts["r"])
    in_specs = ([zspec(256, COL_UP), zspec(256, COL_GA), zspec(256, COL_GB), zspec(256, COL_SB),
                 zspec(256, COL_SC), zspec(256, COL_SH), zspec(512, COL_Q), zspec(512, COL_K),
                 zspec(512, COL_V), zspec(256, COL_QI), zspec(128, COL_KW), tspec(128), tspec(128)]
                + [cspec(w) for w in weights])
    def tspec_out(rows):
        return pl.BlockSpec((1, rows, tt_rows), lambda i, j: (i, 0, j))

    out_shape = [jax.ShapeDtypeStruct((m, 256), F32)] * 3 + [
        jax.ShapeDtypeStruct((m, 512), BF16), jax.ShapeDtypeStruct((n, ATT_W, t), F32),
        jax.ShapeDtypeStruct((m, 512), BF16), jax.ShapeDtypeStruct((n, ATT_W, t), F32),
        jax.ShapeDtypeStruct((n, n_tt, ATT_W, tt_rows), BF16),
        jax.ShapeDtypeStruct((m, IDX_HEADS * IDX_K), BF16),
        jax.ShapeDtypeStruct((m, IDX_K), BF16), jax.ShapeDtypeStruct((n, LANES, t), F32),
        jax.ShapeDtypeStruct((n, POOL_STATE, POOL_W), F32),
        jax.ShapeDtypeStruct((n, CONF_K - 1, CONF_W), F32),
        jax.ShapeDtypeStruct((n, SCONV_K - 1, SCONV_W), F32)]
    out_specs = [ospec(256)] * 3 + [
        ospec(512), tspec_out(ATT_W), ospec(512), tspec_out(ATT_W),
        pl.BlockSpec((1, 1, ATT_W, tt_rows), lambda i, j: (i, j, 0, 0)),
        ospec(IDX_HEADS * IDX_K), ospec(IDX_K), tspec_out(LANES),
        sspec(POOL_STATE, POOL_W), sspec(CONF_K - 1, CONF_W), sspec(SCONV_K - 1, SCONV_W)]
    return pl.pallas_call(
        functools.partial(_mix_prompt_kernel, tt_rows=tt_rows, rc=rc),
        grid=(n, n_tt),
        in_specs=in_specs,
        out_specs=out_specs,
        out_shape=out_shape,
        scratch_shapes=[pltpu.VMEM((16 + tt_rows, POOL_W), F32), pltpu.VMEM((32 + tt_rows, CONF_W), F32),
                        pltpu.VMEM((8 + tt_rows, SCONV_W), F32)],
        compiler_params=pltpu.CompilerParams(dimension_semantics=("parallel", "arbitrary"),
                                             vmem_limit_bytes=VMEM_LIMIT),
        name="mix_prompt",
    )(*([z] * 11), cos, sin, *weights)


def _mix_sample_kernel(zs_ref, sp_ref, scv_ref, ssc_ref, cos_ref, sin_ref,
                       wbd_ref, pscale_ref, cw_ref, cb_ref, cg_ref, cbeta_ref, sw_ref,
                       p_ref, r_ref,
                       a_ref, b_ref, c_ref, qrot_ref, krot_ref, qi3_ref, kwrot_ref,
                       npool_ref, nconf_ref, nsc_ref, *, t_new, past):
    c0 = COL_UP
    n = zs_ref.shape[1]

    def col(t, start, width):
        return zs_ref[t, :, start - c0:start - c0 + width]

    ext_p = [sp_ref[i] for i in range(POOL_STATE)] + [col(t, COL_UP, 256) for t in range(t_new)]
    ext_c = [scv_ref[i] for i in range(CONF_K - 1)] + [
        col(t, COL_GA, 256) * _sigmoid(col(t, COL_GB, 256)) for t in range(t_new)]
    ext_s = [ssc_ref[i] for i in range(SCONV_K - 1)] + [
        col(t, COL_SC, 256) * col(t, COL_SH, 256) for t in range(t_new)]
    lane = lax.broadcasted_iota(I32, (n, POOL_W), 1)
    for t in range(t_new):
        cur = ext_p[POOL_STATE + t]
        acc = cur
        sums = {}
        for j in range(1, 16):
            acc = acc + ext_p[POOL_STATE + t - j]
            if j + 1 in POOL_WINDOWS:
                sums[j + 1] = acc
        pos = jnp.full((n, POOL_W), float(past + t), F32)
        pooled = _pool_select(sums, pos, lane) - cur
        a_ref[t] = _dot(pooled.astype(BF16), wbd_ref[...]) * pscale_ref[...]
        cv = cw_ref[pl.ds(0, 1), :] * ext_c[t]
        for j in range(1, CONF_K):
            cv = cv + cw_ref[pl.ds(j, 1), :] * ext_c[t + j]
        h = _layer_norm(cv + cb_ref[...], cg_ref[...], cbeta_ref[...])
        b_ref[t] = h * _sigmoid(h)
        cc = sw_ref[pl.ds(0, 1), :] * ext_s[t]
        for j in range(1, SCONV_K):
            cc = cc + sw_ref[pl.ds(j, 1), :] * ext_s[t + j]
        c_ref[t] = col(t, COL_SB, 256) * cc
        cos, sin = cos_ref[t], sin_ref[t]
        qrot_ref[t] = _rope(col(t, COL_Q, 512), cos, sin)
        krot_ref[t] = _rope(col(t, COL_K, 512), cos, sin)
        kw = col(t, COL_KW, 128)
        lane_kw = lax.broadcasted_iota(I32, kw.shape, 1)
        kwr = jnp.where(lane_kw < IDX_DIM, _rope(kw, cos, sin), kw)
        kwrot_ref[t] = kwr
        qi3_ref[t] = _place_parts(_split_parts(_rope(col(t, COL_QI, 256), cos, sin)), p_ref).astype(BF16)
    for i in range(POOL_STATE):
        npool_ref[i] = ext_p[t_new + i]
    for i in range(CONF_K - 1):
        nconf_ref[i] = ext_c[t_new + i]
    for i in range(SCONV_K - 1):
        nsc_ref[i] = ext_s[t_new + i]


def _mix_sample(zs_t, sp_t, scv_t, ssc_t, cos, sin, consts, lw, past):
    t_new, n, _ = zs_t.shape
    weights = (lw["wbd"], lw["pool_scale"], lw["conv_w"], lw["conv_b"], lw["conv_ln_g"], lw["conv_ln_b"],
               lw["sconv_w"], consts["p"], consts["r"])
    out_shape = [jax.ShapeDtypeStruct((t_new, n, 256), F32)] * 3 + [
        jax.ShapeDtypeStruct((t_new, n, 512), F32), jax.ShapeDtypeStruct((t_new, n, 512), F32),
        jax.ShapeDtypeStruct((t_new, n, IDX_HEADS * IDX_K), BF16), jax.ShapeDtypeStruct((t_new, n, 128), F32),
        jax.ShapeDtypeStruct((POOL_STATE, n, POOL_W), F32),
        jax.ShapeDtypeStruct((CONF_K - 1, n, CONF_W), F32),
        jax.ShapeDtypeStruct((SCONV_K - 1, n, SCONV_W), F32)]
    return pl.pallas_call(
        functools.partial(_mix_sample_kernel, t_new=t_new, past=past),
        out_shape=out_shape,
        compiler_params=pltpu.CompilerParams(vmem_limit_bytes=VMEM_LIMIT),
        name="mix_sample",
    )(zs_t, sp_t, scv_t, ssc_t, cos, sin, *weights)


def _prompt_block(qb, qbf_ref, qi3_ref, wt_ref, kbf_ref, vtc_ref, ki3_ref, l_ref, o_ref,
                  key_ref, bias_ref, qm_ref, m_ref, den_ref, acc_ref, lg_ref, e_ref, *, bq, ktop):
    kc = bq
    n_act = qb + 1
    krow = lax.broadcasted_iota(I32, (kc, bq), 0)
    qcol = lax.broadcasted_iota(I32, (kc, bq), 1)
    wt = wt_ref[0]

    def row_fold(x):
        parts = [x[i * SUBLANES:(i + 1) * SUBLANES] for i in range(kc // SUBLANES)]
        while len(parts) > 1:
            parts = [parts[i] + parts[i + 1] for i in range(0, len(parts), 2)]
        return parts[0]

    def score_body(c, carry):
        kic = ki3_ref[pl.ds(pl.multiple_of(c * kc, kc), kc), :]
        for h in range(IDX_HEADS):
            lg_ref[h] = _dot_nt(kic, qi3_ref[:, h * IDX_K:(h + 1) * IDX_K])
        acc = None
        for h in range(IDX_HEADS):
            term = jnp.maximum(lg_ref[h] * IDX_DIM ** -0.5, 0.0) * wt[h:h + 1, :]
            acc = term if acc is None else acc + term
        score = acc * IDX_HEADS ** -0.5
        score = jnp.where(c * kc + krow <= qb * bq + qcol, score, -jnp.inf)
        key_ref[c] = _order_key(score)
        return carry
    lax.fori_loop(0, n_act, score_body, 0)

    def count(pred):
        def body(c, part):
            return part + row_fold(jnp.where(pred(key_ref[c]), 1.0, 0.0))
        part = lax.fori_loop(0, n_act, body, jnp.zeros((SUBLANES, bq), F32))
        return jnp.sum(part, axis=0, keepdims=True)

    def search_body(i, t):
        cand = t + lax.shift_left(jnp.int32(1), 31 - i)
        return jnp.where(count(lambda k: k >= cand) >= ktop, cand, t)
    thr = lax.fori_loop(0, 32, search_body, jnp.full((1, bq), INT_MIN, I32))
    need = ktop - count(lambda k: k > thr)

    def tie_body(c, carry):
        halves = []
        for i in range(kc // LANES):
            kk = key_ref[c, i * LANES:(i + 1) * LANES, :]
            eq = kk == thr
            eqf = jnp.where(eq, 1.0, 0.0)
            halves.append((kk, eq, _dot(l_ref[...], eqf.astype(BF16)), jnp.sum(eqf, axis=0, keepdims=True)))
        for i, (kk, eq, rank, total) in enumerate(halves):
            pre = rank + carry
            carry = carry + total
            ok = ((kk > thr) | (eq & (pre <= need))) & (kk > NEG_INF_KEY) & (kk < POS_INF_KEY)
            bias_ref[c, i * LANES:(i + 1) * LANES, :] = jnp.where(ok, 0.0, MASKED)
        return carry
    lax.fori_loop(0, n_act, tie_body, jnp.zeros((1, bq), F32))

    lane2 = lax.broadcasted_iota(I32, (bq, 2 * HEAD_DIM), 1)
    for h in range(N_HEADS):
        p, j = divmod(h, 2)
        qp = qbf_ref[:, p * 2 * HEAD_DIM:(p + 1) * 2 * HEAD_DIM]
        own = (lane2 < HEAD_DIM) if j == 0 else (lane2 >= HEAD_DIM)
        qm_ref[h] = jnp.where(own, qp * HEAD_DIM ** -0.5, jnp.zeros_like(qp)).astype(BF16)
        m_ref[h] = jnp.full((SUBLANES, bq), MASKED, F32)
        den_ref[h] = jnp.zeros((SUBLANES, bq), F32)
        acc_ref[h] = jnp.zeros((HEAD_DIM, bq), F32)

    ahead = 2

    def logits(c, h):
        p = h // 2
        kcb = kbf_ref[pl.ds(pl.multiple_of(c * kc, kc), kc), p * 2 * HEAD_DIM:(p + 1) * 2 * HEAD_DIM]
        lg_ref[h] = _dot_nt(kcb, qm_ref[h]) + bias_ref[c]

    for h in range(ahead):
        logits(0, h)

    def att_body(c, carry):
        c_next = jnp.minimum(c + 1, n_act - 1)

        def softmax(h):
            lg = lg_ref[h]
            m = m_ref[h]
            mn = jnp.maximum(m, jnp.max(lg, axis=0, keepdims=True))
            alpha = jnp.exp(m - mn)
            e = jnp.exp(lg - mn[0:1, :])
            m_ref[h] = mn
            den_ref[h] = alpha * den_ref[h] + jnp.sum(e, axis=0, keepdims=True)
            e_ref[h] = e.astype(BF16)
            acc_ref[h] = alpha[0:1, :] * acc_ref[h]

        def values(h):
            vth = vtc_ref[0, c, h * HEAD_DIM:(h + 1) * HEAD_DIM, :]
            acc_ref[h] += _dot(vth, e_ref[h])

        for h in range(N_HEADS):
            if h + ahead < N_HEADS:
                logits(c, h + ahead)
            else:
                logits(c_next, h + ahead - N_HEADS)
            softmax(h)
            if h >= 1:
                values(h - 1)
        values(N_HEADS - 1)
        return carry
    lax.fori_loop(0, n_act, att_body, 0)
    for p in range(N_HEADS // 2):
        pair = jnp.concatenate([acc_ref[2 * p + j] / den_ref[2 * p + j][0:1, :] for j in range(2)], axis=0)
        o_ref[:, p * 2 * HEAD_DIM:(p + 1) * 2 * HEAD_DIM] = pair.T


def _sample_group(spp, lhs_ref, w_ref, q8_ref, kw8_ref, k8_ref, v8_ref, r_ref, u_ref, o_ref,
                  ki_buf, k_buf, v_buf, ki3t_s, kt_s, vt_s, bias_s, raw_s, *, n_pages, t_new, ktop):
    past = n_pages * PAGE_SIZE
    total = past + LANES
    tp = SUBLANES
    d = IDX_DIM

    def padded(x8):
        return jnp.concatenate([x8, jnp.zeros((LANES - tp, x8.shape[1]), F32)], axis=0).astype(BF16)

    scores, k_new, v_new = [], [], []
    for s in range(spp):
        for j in range(n_pages):
            cols = slice(j * PAGE_SIZE, (j + 1) * PAGE_SIZE)
            parts = _split_parts(ki_buf[s, j])
            for slot in range(IDX_K // d):
                ki3t_s[s, slot * d:(slot + 1) * d, cols] = (
                    parts[IDX_TERMS[slot][1]] if slot < len(IDX_TERMS) else jnp.zeros((d, PAGE_SIZE), BF16))
        ki3_new = padded(_place_parts(_split_parts(kw8_ref[s]), r_ref))
        k_new.append(padded(k8_ref[s]))
        v_new.append(padded(v8_ref[s]))
        lhs = lhs_ref[s]
        sc = jnp.concatenate([_dot(lhs, ki3t_s[s]), _dot_nt(lhs, ki3_new)], axis=1)
        r = jnp.maximum(sc * IDX_DIM ** -0.5, 0.0) * w_ref[s][:, 0:1]
        acc = r[0:tp]
        for h in range(1, IDX_HEADS):
            acc = acc + r[h * tp:(h + 1) * tp]
        scores.append(acc * IDX_HEADS ** -0.5)
    for s in range(spp):
        for j in range(n_pages):
            cols = slice(j * PAGE_SIZE, (j + 1) * PAGE_SIZE)
            kt_s[s, :, cols] = k_buf[s, j].astype(BF16)
            vt_s[s, :, cols] = v_buf[s, j].astype(BF16)
    for s in range(spp):
        q8 = q8_ref[s]
        lane_q = lax.broadcasted_iota(I32, q8.shape, 1)
        qbd = jnp.concatenate([jnp.where((lane_q >= h * HEAD_DIM) & (lane_q < (h + 1) * HEAD_DIM), q8, 0.0)
                               for h in range(N_HEADS)], axis=0).astype(BF16)
        raw_s[s] = jnp.concatenate([_dot(qbd, kt_s[s]), _dot_nt(qbd, k_new[s])], axis=1)
    score = jnp.concatenate(scores, axis=0)
    colp = lax.broadcasted_iota(I32, (spp * tp, total), 1)
    rowp = lax.broadcasted_iota(I32, (spp * tp, total), 0) & (tp - 1)
    visible = (colp < past) | ((colp - past <= rowp) & (colp - past < t_new))
    score = jnp.where(visible, score, -jnp.inf)
    key = _order_key(score)
    thr = _kth_largest_key(key, ktop)
    n_gt = jnp.sum(jnp.where(key > thr, 1.0, 0.0), axis=1, keepdims=True)
    n_eq = jnp.sum(jnp.where(key == thr, 1.0, 0.0), axis=1, keepdims=True)
    fits = (n_eq <= ktop - n_gt) | (thr == NEG_INF_KEY)
    no_split_ties = jnp.min(jnp.where(fits, 1, 0)) == 1

    @pl.when(no_split_ties)
    def _():
        ok = (key >= thr) & (key > NEG_INF_KEY) & (key < POS_INF_KEY)
        bias_s[...] = jnp.where(ok, 0.0, -jnp.inf)

    @pl.when(jnp.logical_not(no_split_ties))
    def _():
        def store(c, b):
            bias_s[:, c * LANES:(c + 1) * LANES] = b
        _selection_bias(key, thr, ktop, u_ref, store)
    probs = []
    for s in range(spp):
        bias = jnp.concatenate([bias_s[s * tp:(s + 1) * tp, :]] * N_HEADS, axis=0)
        lg = raw_s[s] * HEAD_DIM ** -0.5 + bias
        mx = jnp.max(lg, axis=1, keepdims=True)
        e = jnp.exp(lg - mx)
        probs.append((e.astype(BF16), jnp.sum(e, axis=1, keepdims=True)))
    lane = lax.broadcasted_iota(I32, (tp, ATT_W), 1)
    for s in range(spp):
        pb, den = probs[s]
        o_all = (_dot_nt(pb[:, :past], vt_s[s]) + _dot(pb[:, past:], v_new[s])) / den
        out = jnp.zeros((tp, ATT_W), F32)
        for h in range(N_HEADS):
            out = out + jnp.where((lane >= h * HEAD_DIM) & (lane < (h + 1) * HEAD_DIM),
                                  o_all[h * tp:(h + 1) * tp], 0.0)
        o_ref[s] = out


N_PROMPT_IN, N_SAMPLE_IN, N_PROMPT_SCRATCH = 7, 8, 8


def _dsa_kernel(pt_ref, *refs, spp, layer_page0, n_pages, t_new, bq, ktop_prompt, ktop_sample):
    prompt_in = refs[:N_PROMPT_IN]
    sample_in = refs[N_PROMPT_IN:N_PROMPT_IN + N_SAMPLE_IN]
    cki_hbm, ck_hbm, cv_hbm, op_ref, os_ref = refs[N_PROMPT_IN + N_SAMPLE_IN:N_PROMPT_IN + N_SAMPLE_IN + 5]
    scratch = refs[N_PROMPT_IN + N_SAMPLE_IN + 5:]
    prompt_scratch = scratch[:N_PROMPT_SCRATCH]
    ki_buf, k_buf, v_buf, sem, ki3t_s, kt_s, vt_s, bias_s, raw_s = scratch[N_PROMPT_SCRATCH:]
    step = pl.program_id(0) * pl.num_programs(1) + pl.program_id(1)

    def page_copies(slot):
        copies = []
        for j in range(n_pages):
            page = layer_page0 + pt_ref[(step * spp + slot) * n_pages + j]
            copies.append(pltpu.make_async_copy(cki_hbm.at[page], ki_buf.at[slot, j], sem.at[slot, 0]))
            copies.append(pltpu.make_async_copy(ck_hbm.at[page], k_buf.at[slot, j], sem.at[slot, 1]))
            copies.append(pltpu.make_async_copy(cv_hbm.at[page], v_buf.at[slot, j], sem.at[slot, 2]))
        return copies

    for slot in range(spp):
        for cp in page_copies(slot):
            cp.start()
    _prompt_block(pl.program_id(1), *prompt_in, op_ref, *prompt_scratch, bq=bq, ktop=ktop_prompt)
    for slot in range(spp):
        for cp in page_copies(slot):
            cp.wait()
    _sample_group(spp, *sample_in, os_ref, ki_buf, k_buf, v_buf, ki3t_s, kt_s, vt_s, bias_s, raw_s,
                  n_pages=n_pages, t_new=t_new, ktop=ktop_sample)


def _dsa(qbf, qi3, kwt, kbf, vtc, ki3, n, t,
         pt_flat, lhs, wb, q8, kw8, k8, v8, cki, ck, cv, layer, n_phys, n_pages, t_new, consts):
    m = n * t
    nq, bq = vtc.shape[1], vtc.shape[3]
    ktop_prompt = min(TOPK_MAX, t // 4)
    assert bq >= ktop_prompt
    assert WI_LANE % SUBLANES == 0
    ns = q8.shape[0]
    spp = ns // (n * nq)
    assert spp * n * nq == ns
    past = n_pages * PAGE_SIZE
    total = past + LANES
    ktop_sample = min(TOPK_MAX, (past + t_new) // 4)

    def qspec(width):
        return pl.BlockSpec((bq, width), lambda i, j, pt: (i * nq + j, 0))

    def kspec(width):
        return pl.BlockSpec((t, width), lambda i, j, pt: (i, 0))

    def cspec(a):
        return pl.BlockSpec(a.shape, lambda i, j, pt: (0, 0))

    def sspec(a):
        return pl.BlockSpec((spp,) + a.shape[1:], lambda i, j, pt: (i * nq + j, 0, 0))

    hbm = pl.BlockSpec(memory_space=pl.ANY)
    prompt_args = (qbf, qi3, kwt, kbf, vtc, ki3, consts["l"])
    prompt_specs = [qspec(512), qspec(IDX_HEADS * IDX_K),
                    pl.BlockSpec((1, SUBLANES, bq), lambda i, j, pt: (i, WI_LANE // SUBLANES, j)),
                    kspec(512),
                    pl.BlockSpec((1, nq, ATT_W, bq), lambda i, j, pt: (i, 0, 0, 0)),
                    kspec(IDX_K), cspec(consts["l"])]
    sample_args = (lhs, wb, q8, kw8, k8, v8, consts["r"], consts["u"])
    sample_specs = [sspec(a) for a in sample_args[:6]] + [
        pl.BlockSpec(consts["r"].shape, lambda i, j, pt: (0, 0, 0)), cspec(consts["u"])]
    assert len(prompt_args) == N_PROMPT_IN and len(sample_args) == N_SAMPLE_IN
    prompt_scratch = [pltpu.VMEM((nq, bq, bq), I32), pltpu.VMEM((nq, bq, bq), F32),
                      pltpu.VMEM((N_HEADS, bq, 2 * HEAD_DIM), BF16),
                      pltpu.VMEM((N_HEADS, SUBLANES, bq), F32), pltpu.VMEM((N_HEADS, SUBLANES, bq), F32),
                      pltpu.VMEM((N_HEADS, HEAD_DIM, bq), F32),
                      pltpu.VMEM((N_HEADS, bq, bq), F32), pltpu.VMEM((N_HEADS, bq, bq), BF16)]
    assert len(prompt_scratch) == N_PROMPT_SCRATCH
    sample_scratch = [pltpu.VMEM((spp, n_pages, IDX_DIM, PAGE_SIZE), F32),
                      pltpu.VMEM((spp, n_pages, ATT_W, PAGE_SIZE), F32),
                      pltpu.VMEM((spp, n_pages, ATT_W, PAGE_SIZE), F32),
                      pltpu.SemaphoreType.DMA((spp, 3)),
                      pltpu.VMEM((spp, IDX_K, past), BF16), pltpu.VMEM((spp, ATT_W, past), BF16),
                      pltpu.VMEM((spp, ATT_W, past), BF16), pltpu.VMEM((spp * SUBLANES, total), F32),
                      pltpu.VMEM((spp, N_HEADS * SUBLANES, total), F32)]
    grid_spec = pltpu.PrefetchScalarGridSpec(
        num_scalar_prefetch=1,
        grid=(n, nq),
        in_specs=prompt_specs + sample_specs + [hbm, hbm, hbm],
        out_specs=[qspec(512), pl.BlockSpec((spp, SUBLANES, ATT_W), lambda i, j, pt: (i * nq + j, 0, 0))],
        scratch_shapes=prompt_scratch + sample_scratch)
    return pl.pallas_call(
        functools.partial(_dsa_kernel, spp=spp, layer_page0=layer * n_phys, n_pages=n_pages, t_new=t_new,
                          bq=bq, ktop_prompt=ktop_prompt, ktop_sample=ktop_sample),
        grid_spec=grid_spec,
        out_shape=[jax.ShapeDtypeStruct((m, ATT_W), F32), jax.ShapeDtypeStruct((ns, SUBLANES, ATT_W), F32)],
        compiler_params=pltpu.CompilerParams(dimension_semantics=("arbitrary", "arbitrary"),
                                             vmem_limit_bytes=FFN_VMEM_LIMIT),
        name="dsa",
    )(pt_flat, *prompt_args, *sample_args, cki, ck, cv)


def _merge_kernel(a_ref, b_ref, c_ref, d_ref, x_ref, wgate_ref,
                  wa_ref, wb_ref, wc_ref, wd_ref, wo_ref, g_ref, beta_ref, o_ref, *, alpha):
    dm = x_ref.shape[1]
    x = x_ref[...]
    xb = x.astype(BF16)
    merged = None
    for i, (br_ref, w_ref) in enumerate(((a_ref, wa_ref), (b_ref, wb_ref), (c_ref, wc_ref), (d_ref, wd_ref))):
        gate = _sigmoid(_dot(xb, wgate_ref[:, i * dm:(i + 1) * dm]))
        term = gate * _dot(br_ref[...].astype(BF16), w_ref[...])
        merged = term if merged is None else merged + term
    y = _dot(merged.astype(BF16), wo_ref[...])
    o_ref[...] = _layer_norm(alpha * x + y, g_ref[...], beta_ref[...])


def _merge(a, b, c, d, x, lw, alpha):
    m, dm = x.shape
    tm = _pick(m, (512,))
    weights = (lw["w_br_pool"], lw["w_br_conv"], lw["w_br_sconv"], lw["w_br_attn"], lw["w_o"],
               lw["ln1_g"], lw["ln1_b"])

    def rspec(width):
        return pl.BlockSpec((tm, width), lambda i: (i, 0))

    return pl.pallas_call(
        functools.partial(_merge_kernel, alpha=alpha),
        grid=(m // tm,),
        in_specs=([rspec(a.shape[1]), rspec(b.shape[1]), rspec(c.shape[1]), rspec(d.shape[1]), rspec(dm),
                   pl.BlockSpec((dm, N_BRANCH * dm), lambda i: (0, 0))]
                  + [pl.BlockSpec(w.shape, lambda i: (0, 0)) for w in weights]),
        out_specs=rspec(dm),
        out_shape=jax.ShapeDtypeStruct((m, dm), F32),
        compiler_params=pltpu.CompilerParams(dimension_semantics=("parallel",),
                                             vmem_limit_bytes=VMEM_LIMIT),
        name="merge",
    )(a, b, c, d, x, lw["w_in"], *weights)


def _ffn_kernel(x_ref, pe_ref, wg_ref, wu_ref, wdn_ref, wpe_ref, wpg_ref, g_ref, beta_ref, o_ref,
                acc_ref, *, alpha):
    f = pl.program_id(1)

    @pl.when(f == 0)
    def _():
        acc_ref[...] = jnp.zeros_like(acc_ref)

    xb = x_ref[...].astype(BF16)
    tf = wg_ref.shape[1]
    total = None
    for lo in range(0, tf, FFN_SUB):
        hi = min(lo + FFN_SUB, tf)
        hg = _dot(xb, wg_ref[:, lo:hi])
        hu = _dot(xb, wu_ref[:, lo:hi])
        part = _dot((hg * _sigmoid(hg) * hu).astype(BF16), wdn_ref[lo:hi, :])
        total = part if total is None else total + part
    acc_ref[...] += total

    @pl.when(f == pl.num_programs(1) - 1)
    def _():
        r = alpha * x_ref[...] + acc_ref[...]
        emb = _dot(pe_ref[...].astype(BF16), wpe_ref[...])
        r = r + emb * _sigmoid(_dot(r.astype(BF16), wpg_ref[...]))
        o_ref[...] = _layer_norm(r, g_ref[...], beta_ref[...])


def _ffn(x1, pe, lw, alpha):
    m, dm = x1.shape
    d_ff = lw["w_ffn_out"].shape[0]
    tm = _pick(m, (1024, 512))
    tf = _pick(d_ff, (1408, 256))
    nf = d_ff // tf
    consts = (lw["w_pe"], lw["w_pg"], lw["ln2_g"], lw["ln2_b"])
    return pl.pallas_call(
        functools.partial(_ffn_kernel, alpha=alpha),
        grid=(m // tm, nf),
        in_specs=[pl.BlockSpec((tm, dm), lambda i, f: (i, 0)),
                  pl.BlockSpec((tm, pe.shape[1]), lambda i, f: (i, 0)),
                  pl.BlockSpec((dm, tf), lambda i, f: (0, f)),
                  pl.BlockSpec((dm, tf), lambda i, f: (0, nf + f)),
                  pl.BlockSpec((tf, dm), lambda i, f: (f, 0))]
                 + [pl.BlockSpec(w.shape, lambda i, f: (0, 0), pipeline_mode=pl.Buffered(1)) for w in consts],
        out_specs=pl.BlockSpec((tm, dm), lambda i, f: (i, 0)),
        out_shape=jax.ShapeDtypeStruct((m, dm), F32),
        scratch_shapes=[pltpu.VMEM((tm, dm), F32)],
        compiler_params=pltpu.CompilerParams(dimension_semantics=("parallel", "arbitrary"),
                                             vmem_limit_bytes=FFN_VMEM_LIMIT),
        name="ffn",
    )(x1, pe, lw["w_ffn_in"], lw["w_ffn_in"], lw["w_ffn_out"], *consts)


def _selection_constants():
    d, h = IDX_DIM, IDX_HEADS
    p = np.zeros((IDX_PARTS, h * d, h * IDX_K), np.float32)
    r = np.zeros((IDX_PARTS, LANES, IDX_K), np.float32)
    eye = np.eye(d, dtype=np.float32)
    for slot, (qpart, kpart) in enumerate(IDX_TERMS):
        r[kpart, :d, slot * d:(slot + 1) * d] = eye
        for i in range(h):
            p[qpart, i * d:(i + 1) * d, i * IDX_K + slot * d:i * IDX_K + (slot + 1) * d] = eye
    u = np.triu(np.ones((LANES, LANES), np.float32))
    return {k: jnp.asarray(v, BF16) for k, v in dict(p=p, r=r, u=u, l=u.T).items()}


def _rope_tables(pos):
    half = HEAD_DIM // 2
    inv = (np.float32(ROPE_THETA) ** (-np.arange(half, dtype=np.float32) / np.float32(half))).astype(np.float32)
    ang = (pos.astype(np.float32)[:, None] * inv[None, :]).astype(np.float32)
    cos, sin = np.cos(ang).astype(np.float32), np.sin(ang).astype(np.float32)
    cos = np.concatenate([cos, cos] * (LANES // HEAD_DIM), axis=-1)
    sin = np.concatenate([-sin, sin] * (LANES // HEAD_DIM), axis=-1)
    return jnp.asarray(cos), jnp.asarray(sin)


def _layer_weights(i, w_in, pool_mix, pool_scale, conv_w, conv_b, conv_ln_g, conv_ln_b, sconv_w,
                   w_br_pool, w_br_conv, w_br_sconv, w_br_attn, w_o, ln1_g, ln1_b,
                   w_ffn_in, w_ffn_out, w_pe, w_pg, ln2_g, ln2_b):
    def row(v):
        return v[i][None, :]

    def taps(w):
        k = w.shape[1]
        return jnp.pad(w[i], ((0, -k % SUBLANES), (0, 0)))

    n_gate = N_BRANCH * w_in.shape[1]

    wbd = jnp.zeros((POOL_W, POOL_W), F32)
    for g in range(POOL_W // POOL_GROUP_W):
        sl = slice(g * POOL_GROUP_W, (g + 1) * POOL_GROUP_W)
        wbd = wbd.at[sl, sl].set(pool_mix[i, g])
    return dict(
        w_in=jnp.pad(w_in[i], ((0, 0), (0, n_gate + NZ - w_in.shape[2]))).astype(BF16),
        wbd=wbd.astype(BF16), pool_scale=row(pool_scale),
        conv_w=taps(conv_w), conv_b=row(conv_b), conv_ln_g=row(conv_ln_g), conv_ln_b=row(conv_ln_b),
        sconv_w=taps(sconv_w),
        w_br_pool=w_br_pool[i].astype(BF16), w_br_conv=w_br_conv[i].astype(BF16),
        w_br_sconv=w_br_sconv[i].astype(BF16), w_br_attn=w_br_attn[i].astype(BF16),
        w_o=w_o[i].astype(BF16), ln1_g=row(ln1_g), ln1_b=row(ln1_b),
        w_ffn_in=w_ffn_in[i].astype(BF16), w_ffn_out=w_ffn_out[i].astype(BF16),
        w_pe=w_pe[i].astype(BF16), w_pg=w_pg[i].astype(BF16), ln2_g=row(ln2_g), ln2_b=row(ln2_b))


def _pad_rows(x, rows):
    return jnp.pad(x, ((0, 0), (0, rows - x.shape[1]), (0, 0)))


def kernel(x_prompt, x_sample, cache_k, cache_v, cache_kidx, state_pool, state_conv, state_sconv, page_table, p_prompt, p_sample, w_in, pool_mix, pool_scale, conv_w, conv_b, conv_ln_g, conv_ln_b, sconv_w, w_br_pool, w_br_conv, w_br_sconv, w_br_attn, w_o, ln1_g, ln1_b, w_ffn_in, w_ffn_out, w_pe, w_pg, ln2_g, ln2_b):
    nb, seq, dm = x_prompt.shape
    ns, t_new, _ = x_sample.shape
    depth = w_in.shape[0]
    n_phys = cache_k.shape[1]
    n_pages = page_table.shape[1]
    past = n_pages * PAGE_SIZE
    assert t_new <= SUBLANES and w_in.shape[2] == IN_COLS and cache_k.shape[2] == PAGE_SIZE
    alpha = (2 * depth) ** 0.25

    consts = _selection_constants()
    cos_p, sin_p = _rope_tables(np.arange(seq))
    cos_s, sin_s = _rope_tables(past + np.arange(t_new))
    cos_s, sin_s = cos_s[:, None, :], sin_s[:, None, :]
    cki = jnp.transpose(cache_kidx, (0, 1, 3, 2)).reshape(depth * n_phys, IDX_DIM, PAGE_SIZE)
    ck = jnp.transpose(cache_k, (0, 1, 3, 4, 2)).reshape(depth * n_phys, ATT_W, PAGE_SIZE)
    cv = jnp.transpose(cache_v, (0, 1, 3, 4, 2)).reshape(depth * n_phys, ATT_W, PAGE_SIZE)
    pt_flat = page_table.reshape(-1)

    xp = x_prompt.reshape(nb * seq, dm)
    xs = x_sample.reshape(ns * t_new, dm)
    outs = {k: [] for k in ("kp", "vp", "kip", "ks", "vs", "kis", "pp", "ps", "cp", "cs", "sp", "ss")}

    def tm(a):
        return jnp.transpose(a, (1, 0, 2))

    for i in range(depth):
        lw = _layer_weights(i, w_in, pool_mix, pool_scale, conv_w, conv_b, conv_ln_g, conv_ln_b, sconv_w,
                            w_br_pool, w_br_conv, w_br_sconv, w_br_attn, w_o, ln1_g, ln1_b,
                            w_ffn_in, w_ffn_out, w_pe, w_pg, ln2_g, ln2_b)
        zp = _in_proj(xp, lw["w_in"], N_BRANCH * dm)
        (a, b, c, qbf, kt, kbf, vt, vtc, qi3, ki3, kwt, npool, nconf, nsc) = _mix_prompt(
            zp, nb, seq, cos_p, sin_p, consts, lw)
        outs["kp"].append(jnp.transpose(kt.reshape(nb, N_HEADS, HEAD_DIM, seq), (0, 3, 1, 2)))
        outs["vp"].append(jnp.transpose(vt.reshape(nb, N_HEADS, HEAD_DIM, seq), (0, 3, 1, 2)))
        outs["kip"].append(jnp.transpose(kwt[:, :IDX_DIM, :], (0, 2, 1)))
        outs["pp"].append(npool)
        outs["cp"].append(nconf)
        outs["sp"].append(nsc)

        zs = _in_proj(xs, lw["w_in"], N_BRANCH * dm)
        zs_t = tm(zs.reshape(ns, t_new, NZ))
        (a_t, b_t, c_t, q_t, k_t, qi3_t, kw_t, npool_t, nconf_t, nsc_t) = _mix_sample(
            zs_t, tm(state_pool[i]), tm(state_conv[i]), tm(state_sconv[i]), cos_s, sin_s, consts, lw, past)
        q_n, k_n, kw_n = tm(q_t), tm(k_t), tm(kw_t)
        v_n = zs[:, COL_V:COL_V + ATT_W].reshape(ns, t_new, ATT_W)
        lhs = jnp.transpose(qi3_t.reshape(t_new, ns, IDX_HEADS, IDX_K), (1, 2, 0, 3))
        lhs = jnp.pad(lhs, ((0, 0), (0, 0), (0, SUBLANES - t_new), (0, 0))).reshape(ns, IDX_HEADS * SUBLANES, IDX_K)
        wi = jnp.transpose(kw_n[:, :, WI_LANE:WI_LANE + IDX_HEADS], (0, 2, 1))
        wi = jnp.pad(wi, ((0, 0), (0, 0), (0, SUBLANES - t_new))).reshape(ns, IDX_HEADS * SUBLANES, 1)
        wb = jnp.broadcast_to(wi, (ns, IDX_HEADS * SUBLANES, LANES))
        d, d_s = _dsa(qbf, qi3, kwt, kbf, vtc, ki3, nb, seq,
                      pt_flat, lhs, wb, _pad_rows(q_n, SUBLANES), _pad_rows(kw_n, SUBLANES),
                      _pad_rows(k_n, SUBLANES), _pad_rows(v_n, SUBLANES), cki, ck, cv,
                      i, n_phys, n_pages, t_new, consts)
        d_s = d_s[:, :t_new].reshape(ns * t_new, ATT_W)
        x1 = _merge(a, b, c, d, xp, lw, alpha)
        xp = _ffn(x1, p_prompt[i].reshape(nb * seq, -1), lw, alpha)

        def rows(a_tm):
            return tm(a_tm).reshape(ns * t_new, -1)
        x1s = _merge(rows(a_t), rows(b_t), rows(c_t), d_s, xs, lw, alpha)
        xs = _ffn(x1s, p_sample[i].reshape(ns * t_new, -1), lw, alpha)
        outs["ks"].append(k_n.reshape(ns, t_new, N_HEADS, HEAD_DIM))
        outs["vs"].append(v_n.reshape(ns, t_new, N_HEADS, HEAD_DIM))
        outs["kis"].append(kw_n[:, :, :IDX_DIM])
        outs["ps"].append(tm(npool_t))
        outs["cs"].append(tm(nconf_t))
        outs["ss"].append(tm(nsc_t))

    st = {k: jnp.stack(v) for k, v in outs.items()}
    return (xp.reshape(nb, seq, dm), xs.reshape(ns, t_new, dm),
            st["kp"], st["vp"], st["kip"], st["ks"], st["vs"], st["kis"],
            st["pp"], st["ps"], st["cp"], st["cs"], st["sp"], st["ss"])
```

```python
import functools

import numpy as np
import jax
import jax.numpy as jnp
from jax import lax
from jax.experimental import pallas as pl
from jax.experimental.pallas import tpu as pltpu

F32, BF16, I32 = jnp.float32, jnp.bfloat16, jnp.int32

N_BRANCH = 4
POOL_GROUP_W = 64
POOL_W = 256
POOL_WINDOWS = (2, 4, 8, 16)
POOL_STATE = 15
CONF_W = 256
CONF_K = 31
SCONV_W = 256
SCONV_K = 3
N_HEADS = 8
HEAD_DIM = 64
ATT_W = N_HEADS * HEAD_DIM
IDX_HEADS = 4
IDX_DIM = 64
TOPK_MAX = 256
ROPE_THETA = 10000.0
PAGE_SIZE = 128
LN_EPS = 1e-5

LANES = 128
SUBLANES = 8
MXU_DEPTH = 256
VMEM_LIMIT = 56 * 1024 * 1024
FFN_SUB = 768
FFN_VMEM_LIMIT = 60 * 1024 * 1024

COL_UP, COL_GA, COL_GB, COL_SB, COL_SC, COL_SH = 0, 256, 512, 768, 1024, 1280
COL_Q, COL_K, COL_V, COL_QI, COL_KW = 1536, 2048, 2560, 3072, 3328
IN_COLS = 7492
NZ = 3584
PROJ_GROUPS = 7
WI_LANE = IDX_DIM

INT_MIN = -2 ** 31
NEG_INF_KEY = -2139095041
POS_INF_KEY = 0x7F800000
MASKED = -1e30


def _pick(m, prefs):
    for p in prefs:
        if m % p == 0:
            return p
    return m


def _dot(a, b):
    return jnp.dot(a, b, preferred_element_type=F32)


def _dot_nt(a, b):
    return lax.dot_general(a, b, (((1,), (1,)), ((), ())), preferred_element_type=F32)


def _sigmoid(x):
    return jax.nn.sigmoid(x)


def _layer_norm(x, g, b):
    mu = jnp.mean(x, axis=-1, keepdims=True)
    xc = x - mu
    var = jnp.mean(xc * xc, axis=-1, keepdims=True)
    return xc * lax.rsqrt(var + LN_EPS) * g + b


IDX_PARTS = 2
IDX_TERMS = ((0, 0), (1, 0), (0, 1))
IDX_K = MXU_DEPTH


def _split_parts(x):
    parts, rest = [], x
    for i in range(IDX_PARTS):
        part = rest.astype(BF16)
        parts.append(part)
        if i + 1 < IDX_PARTS:
            rest = rest - part.astype(F32)
    return tuple(parts)


def _lane_tile(t, width):
    reps = width // t.shape[-1]
    return t if reps == 1 else jnp.concatenate([t] * reps, axis=-1)


def _rope(x, cos, sin_signed):
    axis = x.ndim - 1
    width = x.shape[-1]
    lane = lax.broadcasted_iota(I32, x.shape, axis)
    first = (lane & (HEAD_DIM // 2)) == 0
    swapped = jnp.where(first, pltpu.roll(x, width - HEAD_DIM // 2, axis), pltpu.roll(x, HEAD_DIM // 2, axis))
    return x * _lane_tile(cos, width) + swapped * _lane_tile(sin_signed, width)


def _order_key(score):
    bits = pltpu.bitcast(score, I32)
    return jnp.where(bits < 0, bits ^ 0x7FFFFFFF, bits)


def _kth_largest_key(key, ktop, digit_bits=2):
    t = jnp.full((key.shape[0], 1), INT_MIN, I32)
    for shift in range(32 - digit_bits, -1, -digit_bits):
        digit = jnp.zeros_like(t)
        for k in range(1, 2 ** digit_bits):
            step = ((k << shift) + 2 ** 31) % 2 ** 32 - 2 ** 31
            cnt = jnp.sum(jnp.where(key >= t + step, 1.0, 0.0), axis=1, keepdims=True)
            digit = digit + jnp.where(cnt >= ktop, 1, 0)
        t = t + digit * (1 << shift)
    return t


def _selection_bias(key, thr, ktop, u_ref, store):
    n_chunks = key.shape[1] // LANES
    gt_cnt = jnp.sum(jnp.where(key > thr, 1.0, 0.0), axis=1, keepdims=True)
    need = ktop - gt_cnt
    carry = jnp.zeros_like(gt_cnt)
    for c in range(n_chunks):
        kc = key[:, c * LANES:(c + 1) * LANES]
        eq = kc == thr
        pre = _dot(jnp.where(eq, 1.0, 0.0).astype(BF16), u_ref[...]) + carry
        carry = pre[:, LANES - 1:LANES]
        sel = (kc > thr) | (eq & (pre <= need))
        ok = sel & (kc > NEG_INF_KEY) & (kc < POS_INF_KEY)
        store(c, jnp.where(ok, 0.0, -jnp.inf))


def _proj_kernel(x_ref, w_ref, o_ref):
    o_ref[...] = _dot(x_ref[...].astype(BF16), w_ref[...])


def _in_proj(x, w, col0):
    m, k = x.shape
    n = w.shape[1] - col0
    tm = _pick(m, (2048, 512))
    tn = 512
    assert col0 % tn == 0 and n % tn == 0
    return pl.pallas_call(
        _proj_kernel,
        grid=(m // tm, n // tn),
        in_specs=[pl.BlockSpec((tm, k), lambda i, j: (i, 0)),
                  pl.BlockSpec((k, tn), lambda i, j: (0, col0 // tn + j))],
        out_specs=pl.BlockSpec((tm, tn), lambda i, j: (i, j)),
        out_shape=jax.ShapeDtypeStruct((m, n), F32),
        compiler_params=pltpu.CompilerParams(dimension_semantics=("parallel", "arbitrary"),
                                             vmem_limit_bytes=VMEM_LIMIT),
        name="in_proj",
    )(x, w)


def _pool_select(sums, pos, lane):
    g = POOL_GROUP_W
    win = jnp.where(lane < g, 2.0, jnp.where(lane < 2 * g, 4.0, jnp.where(lane < 3 * g, 8.0, 16.0)))
    tot = jnp.where(lane < g, sums[2], jnp.where(lane < 2 * g, sums[4], jnp.where(lane < 3 * g, sums[8], sums[16])))
    return tot / jnp.minimum(win, pos + 1.0)


def _place_parts(parts, sel_ref):
    out = _dot(parts[0], sel_ref[0])
    for i in range(1, len(parts)):
        out = out + _dot(parts[i], sel_ref[i])
    return out


def _index_operands(qi_rot, kw_rot, p_ref, r_ref):
    qi3 = _place_parts(_split_parts(qi_rot), p_ref).astype(BF16)
    ki3 = _place_parts(_split_parts(kw_rot), r_ref).astype(BF16)
    return qi3, ki3


def _mix_prompt_kernel(x_ref, *refs, tt_rows, rc):
    w_refs = refs[:PROJ_GROUPS]
    (cos_ref, sin_ref, wbd_ref, pscale_ref, cw_ref, cb_ref, cg_ref, cbeta_ref, sw_ref, p_ref, r_ref,
     a_ref, b_ref, c_ref, qbf_ref, kt_ref, kbf_ref, vt_ref, vtc_ref, qi3_ref, ki3_ref,
     kwt_ref, npool_ref, nconf_ref, nsc_ref, ep, ec, es, z_s) = refs[PROJ_GROUPS:]
    tt = pl.program_id(1)
    n_tt = pl.num_programs(1)
    PP, PC, PS = 16, 32, 8

    @pl.when(tt == 0)
    def _():
        ep[0:PP, :] = jnp.zeros((PP, POOL_W), F32)
        ec[0:PC, :] = jnp.zeros((PC, CONF_W), F32)
        es[0:PS, :] = jnp.zeros((PS, SCONV_W), F32)

    xb = x_ref[...].astype(BF16)
    gw = NZ // PROJ_GROUPS
    n_mixer_groups = COL_Q // gw

    def project(g):
        z_s[:, g * gw:(g + 1) * gw] = _dot(xb, w_refs[g][...])
    for g in range(n_mixer_groups):
        project(g)
    later_groups = list(range(n_mixer_groups, PROJ_GROUPS))

    def zcols(col, width):
        return z_s.at[:, col:col + width]
    up_ref, ga_ref, gb_ref, sb_ref, sc_ref, sh_ref = (zcols(c, 256) for c in (
        COL_UP, COL_GA, COL_GB, COL_SB, COL_SC, COL_SH))
    q_ref, k_ref, v_ref = (zcols(c, ATT_W) for c in (COL_Q, COL_K, COL_V))
    qi_ref, kw_ref = zcols(COL_QI, 256), zcols(COL_KW, LANES)

    ep[PP:PP + tt_rows, :] = up_ref[...]
    ec[PC:PC + tt_rows, :] = ga_ref[...] * _sigmoid(gb_ref[...])
    es[PS:PS + tt_rows, :] = sc_ref[...] * sh_ref[...]

    lane = lax.broadcasted_iota(I32, (rc, POOL_W), 1)
    row = lax.broadcasted_iota(I32, (rc, POOL_W), 0)
    for r in range(0, tt_rows, rc):
        if later_groups:
            project(later_groups.pop(0))
        cur = ep[pl.ds(PP + r, rc), :]
        acc = cur
        sums = {}
        for j in range(1, 16):
            acc = acc + ep[pl.ds(PP + r - j, rc), :]
            if j + 1 in POOL_WINDOWS:
                sums[j + 1] = acc
        pos = (tt * tt_rows + r + row).astype(F32)
        pooled = _pool_select(sums, pos, lane) - cur
        a_ref[pl.ds(r, rc), :] = _dot(pooled.astype(BF16), wbd_ref[...]) * pscale_ref[...]
        cv = cw_ref[pl.ds(0, 1), :] * ec[pl.ds(PC - (CONF_K - 1) + r, rc), :]
        for j in range(1, CONF_K):
            cv = cv + cw_ref[pl.ds(j, 1), :] * ec[pl.ds(PC - (CONF_K - 1) + r + j, rc), :]
        h = _layer_norm(cv + cb_ref[...], cg_ref[...], cbeta_ref[...])
        b_ref[pl.ds(r, rc), :] = h * _sigmoid(h)
        cc = sw_ref[pl.ds(0, 1), :] * es[pl.ds(PS - (SCONV_K - 1) + r, rc), :]
        for j in range(1, SCONV_K):
            cc = cc + sw_ref[pl.ds(j, 1), :] * es[pl.ds(PS - (SCONV_K - 1) + r + j, rc), :]
        c_ref[pl.ds(r, rc), :] = sb_ref[pl.ds(r, rc), :] * cc
    for g in later_groups:
        project(g)

    ep[0:PP, :] = ep[tt_rows:tt_rows + PP, :]
    ec[0:PC, :] = ec[tt_rows:tt_rows + PC, :]
    es[0:PS, :] = es[tt_rows:tt_rows + PS, :]

    cos, sin = cos_ref[...], sin_ref[...]
    qbf_ref[...] = _rope(q_ref[...], cos, sin).astype(BF16)
    kr = _rope(k_ref[...], cos, sin)
    kt_ref[0] = kr.T
    kbf_ref[...] = kr.astype(BF16)
    vt = v_ref[...].T
    vt_ref[0] = vt
    vtc_ref[0, 0] = vt.astype(BF16)
    kw = kw_ref[...]
    lane_kw = lax.broadcasted_iota(I32, kw.shape, 1)
    kwr = jnp.where(lane_kw < IDX_DIM, _rope(kw, cos, sin), kw)
    kwt_ref[0] = kwr.T
    qi3, ki3 = _index_operands(_rope(qi_ref[...], cos, sin), kwr,
                               p_ref, r_ref)
    qi3_ref[...] = qi3
    ki3_ref[...] = ki3

    @pl.when(tt == n_tt - 1)
    def _():
        npool_ref[0] = ep[pl.ds(PP + tt_rows - POOL_STATE, POOL_STATE), :]
        nconf_ref[0] = ec[pl.ds(PC + tt_rows - (CONF_K - 1), CONF_K - 1), :]
        nsc_ref[0] = es[pl.ds(PS + tt_rows - (SCONV_K - 1), SCONV_K - 1), :]


def _mix_prompt(x, col0, n, t, cos, sin, consts, lw):
    m, dm = x.shape
    tt_rows = _pick(t, (256,))
    n_tt = t // tt_rows
    rc = 64

    gw = NZ // PROJ_GROUPS
    assert col0 % gw == 0 and PROJ_GROUPS * gw == NZ

    def wspec(g):
        return pl.BlockSpec((dm, gw), lambda i, j, c=col0 // gw + g: (0, c))

    def tspec(width):
        return pl.BlockSpec((tt_rows, width), lambda i, j: (j, 0))

    def cspec(a):
        return pl.BlockSpec(a.shape, lambda i, j: (0,) * a.ndim)

    def ospec(width):
        return pl.BlockSpec((tt_rows, width), lambda i, j: (i * n_tt + j, 0))

    def sspec(rows, width):
        return pl.BlockSpec((1, rows, width), lambda i, j: (i, 0, 0))

    weights = (lw["wbd"], lw["pool_scale"], lw["conv_w"], lw["conv_b"], lw["conv_ln_g"], lw["conv_ln_b"],
               lw["sconv_w"], consts["p"], consts["r"])
    in_specs = ([pl.BlockSpec((tt_rows, dm), lambda i, j: (i * n_tt + j, 0))]
                + [wspec(g) for g in range(PROJ_GROUPS)] + [tspec(128), tspec(128)]
                + [cspec(w) for w in weights])
    def tspec_out(rows):
        return pl.BlockSpec((1, rows, tt_rows), lambda i, j: (i, 0, j))

    out_shape = [jax.ShapeDtypeStruct((m, 256), F32)] * 3 + [
        jax.ShapeDtypeStruct((m, 512), BF16), jax.ShapeDtypeStruct((n, ATT_W, t), F32),
        jax.ShapeDtypeStruct((m, 512), BF16), jax.ShapeDtypeStruct((n, ATT_W, t), F32),
        jax.ShapeDtypeStruct((n, n_tt, ATT_W, tt_rows), BF16),
        jax.ShapeDtypeStruct((m, IDX_HEADS * IDX_K), BF16),
        jax.ShapeDtypeStruct((m, IDX_K), BF16), jax.ShapeDtypeStruct((n, LANES, t), F32),
        jax.ShapeDtypeStruct((n, POOL_STATE, POOL_W), F32),
        jax.ShapeDtypeStruct((n, CONF_K - 1, CONF_W), F32),
        jax.ShapeDtypeStruct((n, SCONV_K - 1, SCONV_W), F32)]
    out_specs = [ospec(256)] * 3 + [
        ospec(512), tspec_out(ATT_W), ospec(512), tspec_out(ATT_W),
        pl.BlockSpec((1, 1, ATT_W, tt_rows), lambda i, j: (i, j, 0, 0)),
        ospec(IDX_HEADS * IDX_K), ospec(IDX_K), tspec_out(LANES),
        sspec(POOL_STATE, POOL_W), sspec(CONF_K - 1, CONF_W), sspec(SCONV_K - 1, SCONV_W)]
    return pl.pallas_call(
        functools.partial(_mix_prompt_kernel, tt_rows=tt_rows, rc=rc),
        grid=(n, n_tt),
        in_specs=in_specs,
        out_specs=out_specs,
        out_shape=out_shape,
        scratch_shapes=[pltpu.VMEM((16 + tt_rows, POOL_W), F32), pltpu.VMEM((32 + tt_rows, CONF_W), F32),
                        pltpu.VMEM((8 + tt_rows, SCONV_W), F32), pltpu.VMEM((tt_rows, NZ), F32)],
        compiler_params=pltpu.CompilerParams(dimension_semantics=("parallel", "arbitrary"),
                                             vmem_limit_bytes=VMEM_LIMIT),
        name="mix_prompt",
    )(x, *([lw["w_in"]] * PROJ_GROUPS), cos, sin, *weights)


def _mix_sample_kernel(zs_ref, sp_ref, scv_ref, ssc_ref, cos_ref, sin_ref,
                       wbd_ref, pscale_ref, cw_ref, cb_ref, cg_ref, cbeta_ref, sw_ref,
                       p_ref, r_ref,
                       a_ref, b_ref, c_ref, qrot_ref, krot_ref, qi3_ref, kwrot_ref,
                       npool_ref, nconf_ref, nsc_ref, *, t_new, past):
    c0 = COL_UP
    n = zs_ref.shape[1]

    def col(t, start, width):
        return zs_ref[t, :, start - c0:start - c0 + width]

    ext_p = [sp_ref[i] for i in range(POOL_STATE)] + [col(t, COL_UP, 256) for t in range(t_new)]
    ext_c = [scv_ref[i] for i in range(CONF_K - 1)] + [
        col(t, COL_GA, 256) * _sigmoid(col(t, COL_GB, 256)) for t in range(t_new)]
    ext_s = [ssc_ref[i] for i in range(SCONV_K - 1)] + [
        col(t, COL_SC, 256) * col(t, COL_SH, 256) for t in range(t_new)]
    lane = lax.broadcasted_iota(I32, (n, POOL_W), 1)
    for t in range(t_new):
        cur = ext_p[POOL_STATE + t]
        acc = cur
        sums = {}
        for j in range(1, 16):
            acc = acc + ext_p[POOL_STATE + t - j]
            if j + 1 in POOL_WINDOWS:
                sums[j + 1] = acc
        pos = jnp.full((n, POOL_W), float(past + t), F32)
        pooled = _pool_select(sums, pos, lane) - cur
        a_ref[t] = _dot(pooled.astype(BF16), wbd_ref[...]) * pscale_ref[...]
        cv = cw_ref[pl.ds(0, 1), :] * ext_c[t]
        for j in range(1, CONF_K):
            cv = cv + cw_ref[pl.ds(j, 1), :] * ext_c[t + j]
        h = _layer_norm(cv + cb_ref[...], cg_ref[...], cbeta_ref[...])
        b_ref[t] = h * _sigmoid(h)
        cc = sw_ref[pl.ds(0, 1), :] * ext_s[t]
        for j in range(1, SCONV_K):
            cc = cc + sw_ref[pl.ds(j, 1), :] * ext_s[t + j]
        c_ref[t] = col(t, COL_SB, 256) * cc
        cos, sin = cos_ref[t], sin_ref[t]
        qrot_ref[t] = _rope(col(t, COL_Q, 512), cos, sin)
        krot_ref[t] = _rope(col(t, COL_K, 512), cos, sin)
        kw = col(t, COL_KW, 128)
        lane_kw = lax.broadcasted_iota(I32, kw.shape, 1)
        kwr = jnp.where(lane_kw < IDX_DIM, _rope(kw, cos, sin), kw)
        kwrot_ref[t] = kwr
        qi3_ref[t] = _place_parts(_split_parts(_rope(col(t, COL_QI, 256), cos, sin)), p_ref).astype(BF16)
    for i in range(POOL_STATE):
        npool_ref[i] = ext_p[t_new + i]
    for i in range(CONF_K - 1):
        nconf_ref[i] = ext_c[t_new + i]
    for i in range(SCONV_K - 1):
        nsc_ref[i] = ext_s[t_new + i]


def _mix_sample(zs_t, sp_t, scv_t, ssc_t, cos, sin, consts, lw, past):
    t_new, n, _ = zs_t.shape
    weights = (lw["wbd"], lw["pool_scale"], lw["conv_w"], lw["conv_b"], lw["conv_ln_g"], lw["conv_ln_b"],
               lw["sconv_w"], consts["p"], consts["r"])
    out_shape = [jax.ShapeDtypeStruct((t_new, n, 256), F32)] * 3 + [
        jax.ShapeDtypeStruct((t_new, n, 512), F32), jax.ShapeDtypeStruct((t_new, n, 512), F32),
        jax.ShapeDtypeStruct((t_new, n, IDX_HEADS * IDX_K), BF16), jax.ShapeDtypeStruct((t_new, n, 128), F32),
        jax.ShapeDtypeStruct((POOL_STATE, n, POOL_W), F32),
        jax.ShapeDtypeStruct((CONF_K - 1, n, CONF_W), F32),
        jax.ShapeDtypeStruct((SCONV_K - 1, n, SCONV_W), F32)]
    return pl.pallas_call(
        functools.partial(_mix_sample_kernel, t_new=t_new, past=past),
        out_shape=out_shape,
        compiler_params=pltpu.CompilerParams(vmem_limit_bytes=VMEM_LIMIT),
        name="mix_sample",
    )(zs_t, sp_t, scv_t, ssc_t, cos, sin, *weights)


def _prompt_block(qb, qbf_ref, qi3_ref, wt_ref, kbf_ref, vtc_ref, ki3_ref, l_ref, o_ref,
                  key_ref, bias_ref, qm_ref, m_ref, den_ref, acc_ref, lg_ref, e_ref, *, bq, ktop):
    kc = bq
    n_act = qb + 1
    krow = lax.broadcasted_iota(I32, (kc, bq), 0)
    qcol = lax.broadcasted_iota(I32, (kc, bq), 1)
    wt = wt_ref[0]

    def row_fold(x):
        parts = [x[i * SUBLANES:(i + 1) * SUBLANES] for i in range(kc // SUBLANES)]
        while len(parts) > 1:
            parts = [parts[i] + parts[i + 1] for i in range(0, len(parts), 2)]
        return parts[0]

    def score_body(c, carry):
        kic = ki3_ref[pl.ds(pl.multiple_of(c * kc, kc), kc), :]
        for h in range(IDX_HEADS):
            lg_ref[h] = _dot_nt(kic, qi3_ref[:, h * IDX_K:(h + 1) * IDX_K])
        acc = None
        for h in range(IDX_HEADS):
            term = jnp.maximum(lg_ref[h] * IDX_DIM ** -0.5, 0.0) * wt[h:h + 1, :]
            acc = term if acc is None else acc + term
        score = acc * IDX_HEADS ** -0.5
        score = jnp.where(c * kc + krow <= qb * bq + qcol, score, -jnp.inf)
        key_ref[c] = _order_key(score)
        return carry
    lax.fori_loop(0, n_act, score_body, 0)

    def count(pred):
        def body(c, part):
            return part + row_fold(jnp.where(pred(key_ref[c]), 1.0, 0.0))
        part = lax.fori_loop(0, n_act, body, jnp.zeros((SUBLANES, bq), F32))
        return jnp.sum(part, axis=0, keepdims=True)

    def search_body(i, t):
        cand = t + lax.shift_left(jnp.int32(1), 31 - i)
        return jnp.where(count(lambda k: k >= cand) >= ktop, cand, t)
    n_bits = jnp.where((qb + 1) * bq <= ktop, 0, 32)
    thr = lax.fori_loop(0, n_bits, search_body, jnp.full((1, bq), INT_MIN, I32))
    need = ktop - count(lambda k: k > thr)

    def tie_body(c, carry):
        halves = []
        for i in range(kc // LANES):
            kk = key_ref[c, i * LANES:(i + 1) * LANES, :]
            eq = kk == thr
            eqf = jnp.where(eq, 1.0, 0.0)
            halves.append((kk, eq, _dot(l_ref[...], eqf.astype(BF16)), jnp.sum(eqf, axis=0, keepdims=True)))
        for i, (kk, eq, rank, total) in enumerate(halves):
            pre = rank + carry
            carry = carry + total
            ok = ((kk > thr) | (eq & (pre <= need))) & (kk > NEG_INF_KEY) & (kk < POS_INF_KEY)
            bias_ref[c, i * LANES:(i + 1) * LANES, :] = jnp.where(ok, 0.0, MASKED)
        return carry
    lax.fori_loop(0, n_act, tie_body, jnp.zeros((1, bq), F32))

    lane2 = lax.broadcasted_iota(I32, (bq, 2 * HEAD_DIM), 1)
    for h in range(N_HEADS):
        p, j = divmod(h, 2)
        qp = qbf_ref[:, p * 2 * HEAD_DIM:(p + 1) * 2 * HEAD_DIM]
        own = (lane2 < HEAD_DIM) if j == 0 else (lane2 >= HEAD_DIM)
        qm_ref[h] = jnp.where(own, qp * HEAD_DIM ** -0.5, jnp.zeros_like(qp)).astype(BF16)
        m_ref[h] = jnp.full((SUBLANES, bq), MASKED, F32)
        den_ref[h] = jnp.zeros((SUBLANES, bq), F32)
        acc_ref[h] = jnp.zeros((HEAD_DIM, bq), F32)

    ahead = 2

    def logits(c, h):
        p = h // 2
        kcb = kbf_ref[pl.ds(pl.multiple_of(c * kc, kc), kc), p * 2 * HEAD_DIM:(p + 1) * 2 * HEAD_DIM]
        lg_ref[h] = _dot_nt(kcb, qm_ref[h]) + bias_ref[c]

    for h in range(ahead):
        logits(0, h)

    def att_body(c, carry):
        c_next = jnp.minimum(c + 1, n_act - 1)

        def softmax(h):
            lg = lg_ref[h]
            m = m_ref[h]
            mn = jnp.maximum(m, jnp.max(lg, axis=0, keepdims=True))
            alpha = jnp.exp(m - mn)
            e = jnp.exp(lg - mn[0:1, :])
            m_ref[h] = mn
            den_ref[h] = alpha * den_ref[h] + jnp.sum(e, axis=0, keepdims=True)
            e_ref[h] = e.astype(BF16)
            acc_ref[h] = alpha[0:1, :] * acc_ref[h]

        def values(h):
            vth = vtc_ref[0, c, h * HEAD_DIM:(h + 1) * HEAD_DIM, :]
            acc_ref[h] += _dot(vth, e_ref[h])

        for h in range(N_HEADS):
            if h + ahead < N_HEADS:
                logits(c, h + ahead)
            else:
                logits(c_next, h + ahead - N_HEADS)
            softmax(h)
            if h >= 1:
                values(h - 1)
        values(N_HEADS - 1)
        return carry
    lax.fori_loop(0, n_act, att_body, 0)
    for p in range(N_HEADS // 2):
        pair = jnp.concatenate([acc_ref[2 * p + j] / den_ref[2 * p + j][0:1, :] for j in range(2)], axis=0)
        o_ref[:, p * 2 * HEAD_DIM:(p + 1) * 2 * HEAD_DIM] = pair.T


def _sample_group(spp, lhs_ref, w_ref, q8_ref, kw8_ref, k8_ref, v8_ref, r_ref, u_ref, o_ref,
                  ki_buf, k_buf, v_buf, ki3t_s, kt_s, vt_s, bias_s, raw_s, *, n_pages, t_new, ktop):
    past = n_pages * PAGE_SIZE
    total = past + LANES
    tp = SUBLANES
    d = IDX_DIM

    def padded(x8):
        return jnp.concatenate([x8, jnp.zeros((LANES - tp, x8.shape[1]), F32)], axis=0).astype(BF16)

    scores, k_new, v_new = [], [], []
    for s in range(spp):
        for j in range(n_pages):
            cols = slice(j * PAGE_SIZE, (j + 1) * PAGE_SIZE)
            parts = _split_parts(ki_buf[s, j])
            for slot in range(IDX_K // d):
                ki3t_s[s, slot * d:(slot + 1) * d, cols] = (
                    parts[IDX_TERMS[slot][1]] if slot < len(IDX_TERMS) else jnp.zeros((d, PAGE_SIZE), BF16))
        ki3_new = padded(_place_parts(_split_parts(kw8_ref[s]), r_ref))
        k_new.append(padded(k8_ref[s]))
        v_new.append(padded(v8_ref[s]))
        lhs = lhs_ref[s]
        sc = jnp.concatenate([_dot(lhs, ki3t_s[s]), _dot_nt(lhs, ki3_new)], axis=1)
        r = jnp.maximum(sc * IDX_DIM ** -0.5, 0.0) * w_ref[s][:, 0:1]
        acc = r[0:tp]
        for h in range(1, IDX_HEADS):
            acc = acc + r[h * tp:(h + 1) * tp]
        scores.append(acc * IDX_HEADS ** -0.5)
    for s in range(spp):
        for j in range(n_pages):
            cols = slice(j * PAGE_SIZE, (j + 1) * PAGE_SIZE)
            kt_s[s, :, cols] = k_buf[s, j].astype(BF16)
            vt_s[s, :, cols] = v_buf[s, j].astype(BF16)
    for s in range(spp):
        q8 = q8_ref[s]
        lane_q = lax.broadcasted_iota(I32, q8.shape, 1)
        qbd = jnp.concatenate([jnp.where((lane_q >= h * HEAD_DIM) & (lane_q < (h + 1) * HEAD_DIM), q8, 0.0)
                               for h in range(N_HEADS)], axis=0).astype(BF16)
        raw_s[s] = jnp.concatenate([_dot(qbd, kt_s[s]), _dot_nt(qbd, k_new[s])], axis=1)
    score = jnp.concatenate(scores, axis=0)
    colp = lax.broadcasted_iota(I32, (spp * tp, total), 1)
    rowp = lax.broadcasted_iota(I32, (spp * tp, total), 0) & (tp - 1)
    visible = (colp < past) | ((colp - past <= rowp) & (colp - past < t_new))
    score = jnp.where(visible, score, -jnp.inf)
    key = _order_key(score)
    thr = _kth_largest_key(key, ktop)
    n_gt = jnp.sum(jnp.where(key > thr, 1.0, 0.0), axis=1, keepdims=True)
    n_eq = jnp.sum(jnp.where(key == thr, 1.0, 0.0), axis=1, keepdims=True)
    fits = (n_eq <= ktop - n_gt) | (thr == NEG_INF_KEY)
    no_split_ties = jnp.min(jnp.where(fits, 1, 0)) == 1

    @pl.when(no_split_ties)
    def _():
        ok = (key >= thr) & (key > NEG_INF_KEY) & (key < POS_INF_KEY)
        bias_s[...] = jnp.where(ok, 0.0, -jnp.inf)

    @pl.when(jnp.logical_not(no_split_ties))
    def _():
        def store(c, b):
            bias_s[:, c * LANES:(c + 1) * LANES] = b
        _selection_bias(key, thr, ktop, u_ref, store)
    probs = []
    for s in range(spp):
        bias = jnp.concatenate([bias_s[s * tp:(s + 1) * tp, :]] * N_HEADS, axis=0)
        lg = raw_s[s] * HEAD_DIM ** -0.5 + bias
        mx = jnp.max(lg, axis=1, keepdims=True)
        e = jnp.exp(lg - mx)
        probs.append((e.astype(BF16), jnp.sum(e, axis=1, keepdims=True)))
    lane = lax.broadcasted_iota(I32, (tp, ATT_W), 1)
    for s in range(spp):
        pb, den = probs[s]
        o_all = (_dot_nt(pb[:, :past], vt_s[s]) + _dot(pb[:, past:], v_new[s])) / den
        out = jnp.zeros((tp, ATT_W), F32)
        for h in range(N_HEADS):
            out = out + jnp.where((lane >= h * HEAD_DIM) & (lane < (h + 1) * HEAD_DIM),
                                  o_all[h * tp:(h + 1) * tp], 0.0)
        o_ref[s] = out


N_PROMPT_IN, N_SAMPLE_IN, N_PROMPT_SCRATCH = 7, 8, 8


def _dsa_kernel(pt_ref, *refs, spp, layer_page0, n_pages, t_new, bq, ktop_prompt, ktop_sample):
    prompt_in = refs[:N_PROMPT_IN]
    sample_in = refs[N_PROMPT_IN:N_PROMPT_IN + N_SAMPLE_IN]
    cki_hbm, ck_hbm, cv_hbm, op_ref, os_ref = refs[N_PROMPT_IN + N_SAMPLE_IN:N_PROMPT_IN + N_SAMPLE_IN + 5]
    scratch = refs[N_PROMPT_IN + N_SAMPLE_IN + 5:]
    prompt_scratch = scratch[:N_PROMPT_SCRATCH]
    ki_buf, k_buf, v_buf, sem, ki3t_s, kt_s, vt_s, bias_s, raw_s = scratch[N_PROMPT_SCRATCH:]
    step = pl.program_id(0) * pl.num_programs(1) + pl.program_id(1)

    def page_copies(slot):
        copies = []
        for j in range(n_pages):
            page = layer_page0 + pt_ref[(step * spp + slot) * n_pages + j]
            copies.append(pltpu.make_async_copy(cki_hbm.at[page], ki_buf.at[slot, j], sem.at[slot, 0]))
            copies.append(pltpu.make_async_copy(ck_hbm.at[page], k_buf.at[slot, j], sem.at[slot, 1]))
            copies.append(pltpu.make_async_copy(cv_hbm.at[page], v_buf.at[slot, j], sem.at[slot, 2]))
        return copies

    for slot in range(spp):
        for cp in page_copies(slot):
            cp.start()
    _prompt_block(pl.program_id(1), *prompt_in, op_ref, *prompt_scratch, bq=bq, ktop=ktop_prompt)
    for slot in range(spp):
        for cp in page_copies(slot):
            cp.wait()
    _sample_group(spp, *sample_in, os_ref, ki_buf, k_buf, v_buf, ki3t_s, kt_s, vt_s, bias_s, raw_s,
                  n_pages=n_pages, t_new=t_new, ktop=ktop_sample)


def _dsa(qbf, qi3, kwt, kbf, vtc, ki3, n, t,
         pt_flat, lhs, wb, q8, kw8, k8, v8, cki, ck, cv, layer, n_phys, n_pages, t_new, consts):
    m = n * t
    nq, bq = vtc.shape[1], vtc.shape[3]
    ktop_prompt = min(TOPK_MAX, t // 4)
    assert bq >= ktop_prompt
    assert WI_LANE % SUBLANES == 0
    ns = q8.shape[0]
    spp = ns // (n * nq)
    assert spp * n * nq == ns
    past = n_pages * PAGE_SIZE
    total = past + LANES
    ktop_sample = min(TOPK_MAX, (past + t_new) // 4)

    def qspec(width):
        return pl.BlockSpec((bq, width), lambda i, j, pt: (i * nq + j, 0))

    def kspec(width):
        return pl.BlockSpec((t, width), lambda i, j, pt: (i, 0))

    def cspec(a):
        return pl.BlockSpec(a.shape, lambda i, j, pt: (0, 0))

    def sspec(a):
        return pl.BlockSpec((spp,) + a.shape[1:], lambda i, j, pt: (i * nq + j, 0, 0))

    hbm = pl.BlockSpec(memory_space=pl.ANY)
    prompt_args = (qbf, qi3, kwt, kbf, vtc, ki3, consts["l"])
    prompt_specs = [qspec(512), qspec(IDX_HEADS * IDX_K),
                    pl.BlockSpec((1, SUBLANES, bq), lambda i, j, pt: (i, WI_LANE // SUBLANES, j)),
                    kspec(512),
                    pl.BlockSpec((1, nq, ATT_W, bq), lambda i, j, pt: (i, 0, 0, 0)),
                    kspec(IDX_K), cspec(consts["l"])]
    sample_args = (lhs, wb, q8, kw8, k8, v8, consts["r"], consts["u"])
    sample_specs = [sspec(a) for a in sample_args[:6]] + [
        pl.BlockSpec(consts["r"].shape, lambda i, j, pt: (0, 0, 0)), cspec(consts["u"])]
    assert len(prompt_args) == N_PROMPT_IN and len(sample_args) == N_SAMPLE_IN
    prompt_scratch = [pltpu.VMEM((nq, bq, bq), I32), pltpu.VMEM((nq, bq, bq), F32),
                      pltpu.VMEM((N_HEADS, bq, 2 * HEAD_DIM), BF16),
                      pltpu.VMEM((N_HEADS, SUBLANES, bq), F32), pltpu.VMEM((N_HEADS, SUBLANES, bq), F32),
                      pltpu.VMEM((N_HEADS, HEAD_DIM, bq), F32),
                      pltpu.VMEM((N_HEADS, bq, bq), F32), pltpu.VMEM((N_HEADS, bq, bq), BF16)]
    assert len(prompt_scratch) == N_PROMPT_SCRATCH
    sample_scratch = [pltpu.VMEM((spp, n_pages, IDX_DIM, PAGE_SIZE), F32),
                      pltpu.VMEM((spp, n_pages, ATT_W, PAGE_SIZE), F32),
                      pltpu.VMEM((spp, n_pages, ATT_W, PAGE_SIZE), F32),
                      pltpu.SemaphoreType.DMA((spp, 3)),
                      pltpu.VMEM((spp, IDX_K, past), BF16), pltpu.VMEM((spp, ATT_W, past), BF16),
                      pltpu.VMEM((spp, ATT_W, past), BF16), pltpu.VMEM((spp * SUBLANES, total), F32),
                      pltpu.VMEM((spp, N_HEADS * SUBLANES, total), F32)]
    grid_spec = pltpu.PrefetchScalarGridSpec(
        num_scalar_prefetch=1,
        grid=(n, nq),
        in_specs=prompt_specs + sample_specs + [hbm, hbm, hbm],
        out_specs=[qspec(512), pl.BlockSpec((spp, SUBLANES, ATT_W), lambda i, j, pt: (i * nq + j, 0, 0))],
        scratch_shapes=prompt_scratch + sample_scratch)
    return pl.pallas_call(
        functools.partial(_dsa_kernel, spp=spp, layer_page0=layer * n_phys, n_pages=n_pages, t_new=t_new,
                          bq=bq, ktop_prompt=ktop_prompt, ktop_sample=ktop_sample),
        grid_spec=grid_spec,
        out_shape=[jax.ShapeDtypeStruct((m, ATT_W), F32), jax.ShapeDtypeStruct((ns, SUBLANES, ATT_W), F32)],
        compiler_params=pltpu.CompilerParams(dimension_semantics=("arbitrary", "arbitrary"),
                                             vmem_limit_bytes=FFN_VMEM_LIMIT),
        name="dsa",
    )(pt_flat, *prompt_args, *sample_args, cki, ck, cv)


def _merge_kernel(a_ref, b_ref, c_ref, d_ref, x_ref, wgate_ref,
                  wa_ref, wb_ref, wc_ref, wd_ref, wo_ref, g_ref, beta_ref, o_ref, *, alpha):
    dm = x_ref.shape[1]
    x = x_ref[...]
    xb = x.astype(BF16)
    merged = None
    for i, (br_ref, w_ref) in enumerate(((a_ref, wa_ref), (b_ref, wb_ref), (c_ref, wc_ref), (d_ref, wd_ref))):
        gate = _sigmoid(_dot(xb, wgate_ref[:, i * dm:(i + 1) * dm]))
        term = gate * _dot(br_ref[...].astype(BF16), w_ref[...])
        merged = term if merged is None else merged + term
    y = _dot(merged.astype(BF16), wo_ref[...])
    o_ref[...] = _layer_norm(alpha * x + y, g_ref[...], beta_ref[...])


def _merge(a, b, c, d, x, lw, alpha):
    m, dm = x.shape
    tm = _pick(m, (512,))
    weights = (lw["w_br_pool"], lw["w_br_conv"], lw["w_br_sconv"], lw["w_br_attn"], lw["w_o"],
               lw["ln1_g"], lw["ln1_b"])

    def rspec(width):
        return pl.BlockSpec((tm, width), lambda i: (i, 0))

    return pl.pallas_call(
        functools.partial(_merge_kernel, alpha=alpha),
        grid=(m // tm,),
        in_specs=([rspec(a.shape[1]), rspec(b.shape[1]), rspec(c.shape[1]), rspec(d.shape[1]), rspec(dm),
                   pl.BlockSpec((dm, N_BRANCH * dm), lambda i: (0, 0))]
                  + [pl.BlockSpec(w.shape, lambda i: (0, 0)) for w in weights]),
        out_specs=rspec(dm),
        out_shape=jax.ShapeDtypeStruct((m, dm), F32),
        compiler_params=pltpu.CompilerParams(dimension_semantics=("parallel",),
                                             vmem_limit_bytes=VMEM_LIMIT),
        name="merge",
    )(a, b, c, d, x, lw["w_in"], *weights)


def _ffn_kernel(x_ref, pe_ref, wg_ref, wu_ref, wdn_ref, wpe_ref, wpg_ref, g_ref, beta_ref, o_ref,
                acc_ref, *, alpha):
    f = pl.program_id(1)

    @pl.when(f == 0)
    def _():
        acc_ref[...] = jnp.zeros_like(acc_ref)

    xb = x_ref[...].astype(BF16)
    tf = wg_ref.shape[1]
    total = None
    for lo in range(0, tf, FFN_SUB):
        hi = min(lo + FFN_SUB, tf)
        hg = _dot(xb, wg_ref[:, lo:hi])
        hu = _dot(xb, wu_ref[:, lo:hi])
        part = _dot((hg * _sigmoid(hg) * hu).astype(BF16), wdn_ref[lo:hi, :])
        total = part if total is None else total + part
    acc_ref[...] += total

    @pl.when(f == pl.num_programs(1) - 1)
    def _():
        r = alpha * x_ref[...] + acc_ref[...]
        emb = _dot(pe_ref[...].astype(BF16), wpe_ref[...])
        r = r + emb * _sigmoid(_dot(r.astype(BF16), wpg_ref[...]))
        o_ref[...] = _layer_norm(r, g_ref[...], beta_ref[...])


def _ffn(x1, pe, lw, alpha):
    m, dm = x1.shape
    d_ff = lw["w_ffn_out"].shape[0]
    tm = _pick(m, (1024, 512))
    tf = _pick(d_ff, (1408, 256))
    nf = d_ff // tf
    consts = (lw["w_pe"], lw["w_pg"], lw["ln2_g"], lw["ln2_b"])
    return pl.pallas_call(
        functools.partial(_ffn_kernel, alpha=alpha),
        grid=(m // tm, nf),
        in_specs=[pl.BlockSpec((tm, dm), lambda i, f: (i, 0)),
                  pl.BlockSpec((tm, pe.shape[1]), lambda i, f: (i, 0)),
                  pl.BlockSpec((dm, tf), lambda i, f: (0, f)),
                  pl.BlockSpec((dm, tf), lambda i, f: (0, nf + f)),
                  pl.BlockSpec((tf, dm), lambda i, f: (f, 0))]
                 + [pl.BlockSpec(w.shape, lambda i, f: (0, 0), pipeline_mode=pl.Buffered(1)) for w in consts],
        out_specs=pl.BlockSpec((tm, dm), lambda i, f: (i, 0)),
        out_shape=jax.ShapeDtypeStruct((m, dm), F32),
        scratch_shapes=[pltpu.VMEM((tm, dm), F32)],
        compiler_params=pltpu.CompilerParams(dimension_semantics=("parallel", "arbitrary"),
                                             vmem_limit_bytes=FFN_VMEM_LIMIT),
        name="ffn",
    )(x1, pe, lw["w_ffn_in"], lw["w_ffn_in"], lw["w_ffn_out"], *consts)


def _selection_constants():
    d, h = IDX_DIM, IDX_HEADS
    p = np.zeros((IDX_PARTS, h * d, h * IDX_K), np.float32)
    r = np.zeros((IDX_PARTS, LANES, IDX_K), np.float32)
    eye = np.eye(d, dtype=np.float32)
    for slot, (qpart, kpart) in enumerate(IDX_TERMS):
        r[kpart, :d, slot * d:(slot + 1) * d] = eye
        for i in range(h):
            p[qpart, i * d:(i + 1) * d, i * IDX_K + slot * d:i * IDX_K + (slot + 1) * d] = eye
    u = np.triu(np.ones((LANES, LANES), np.float32))
    return {k: jnp.asarray(v, BF16) for k, v in dict(p=p, r=r, u=u, l=u.T).items()}


def _rope_tables(pos):
    half = HEAD_DIM // 2
    inv = (np.float32(ROPE_THETA) ** (-np.arange(half, dtype=np.float32) / np.float32(half))).astype(np.float32)
    ang = (pos.astype(np.float32)[:, None] * inv[None, :]).astype(np.float32)
    cos, sin = np.cos(ang).astype(np.float32), np.sin(ang).astype(np.float32)
    cos = np.concatenate([cos, cos] * (LANES // HEAD_DIM), axis=-1)
    sin = np.concatenate([-sin, sin] * (LANES // HEAD_DIM), axis=-1)
    return jnp.asarray(cos), jnp.asarray(sin)


def _layer_weights(i, w_in, pool_mix, pool_scale, conv_w, conv_b, conv_ln_g, conv_ln_b, sconv_w,
                   w_br_pool, w_br_conv, w_br_sconv, w_br_attn, w_o, ln1_g, ln1_b,
                   w_ffn_in, w_ffn_out, w_pe, w_pg, ln2_g, ln2_b):
    def row(v):
        return v[i][None, :]

    def taps(w):
        k = w.shape[1]
        return jnp.pad(w[i], ((0, -k % SUBLANES), (0, 0)))

    n_gate = N_BRANCH * w_in.shape[1]

    wbd = jnp.zeros((POOL_W, POOL_W), F32)
    for g in range(POOL_W // POOL_GROUP_W):
        sl = slice(g * POOL_GROUP_W, (g + 1) * POOL_GROUP_W)
        wbd = wbd.at[sl, sl].set(pool_mix[i, g])
    return dict(
        w_in=jnp.pad(w_in[i], ((0, 0), (0, n_gate + NZ - w_in.shape[2]))).astype(BF16),
        wbd=wbd.astype(BF16), pool_scale=row(pool_scale),
        conv_w=taps(conv_w), conv_b=row(conv_b), conv_ln_g=row(conv_ln_g), conv_ln_b=row(conv_ln_b),
        sconv_w=taps(sconv_w),
        w_br_pool=w_br_pool[i].astype(BF16), w_br_conv=w_br_conv[i].astype(BF16),
        w_br_sconv=w_br_sconv[i].astype(BF16), w_br_attn=w_br_attn[i].astype(BF16),
        w_o=w_o[i].astype(BF16), ln1_g=row(ln1_g), ln1_b=row(ln1_b),
        w_ffn_in=w_ffn_in[i].astype(BF16), w_ffn_out=w_ffn_out[i].astype(BF16),
        w_pe=w_pe[i].astype(BF16), w_pg=w_pg[i].astype(BF16), ln2_g=row(ln2_g), ln2_b=row(ln2_b))


def _pad_rows(x, rows):
    return jnp.pad(x, ((0, 0), (0, rows - x.shape[1]), (0, 0)))


def kernel(x_prompt, x_sample, cache_k, cache_v, cache_kidx, state_pool, state_conv, state_sconv, page_table, p_prompt, p_sample, w_in, pool_mix, pool_scale, conv_w, conv_b, conv_ln_g, conv_ln_b, sconv_w, w_br_pool, w_br_conv, w_br_sconv, w_br_attn, w_o, ln1_g, ln1_b, w_ffn_in, w_ffn_out, w_pe, w_pg, ln2_g, ln2_b):
    nb, seq, dm = x_prompt.shape
    ns, t_new, _ = x_sample.shape
    depth = w_in.shape[0]
    n_phys = cache_k.shape[1]
    n_pages = page_table.shape[1]
    past = n_pages * PAGE_SIZE
    assert t_new <= SUBLANES and w_in.shape[2] == IN_COLS and cache_k.shape[2] == PAGE_SIZE
    alpha = (2 * depth) ** 0.25

    consts = _selection_constants()
    cos_p, sin_p = _rope_tables(np.arange(seq))
    cos_s, sin_s = _rope_tables(past + np.arange(t_new))
    cos_s, sin_s = cos_s[:, None, :], sin_s[:, None, :]
    cki = jnp.transpose(cache_kidx, (0, 1, 3, 2)).reshape(depth * n_phys, IDX_DIM, PAGE_SIZE)
    ck = jnp.transpose(cache_k, (0, 1, 3, 4, 2)).reshape(depth * n_phys, ATT_W, PAGE_SIZE)
    cv = jnp.transpose(cache_v, (0, 1, 3, 4, 2)).reshape(depth * n_phys, ATT_W, PAGE_SIZE)
    pt_flat = page_table.reshape(-1)

    xp = x_prompt.reshape(nb * seq, dm)
    xs = x_sample.reshape(ns * t_new, dm)
    outs = {k: [] for k in ("kp", "vp", "kip", "ks", "vs", "kis", "pp", "ps", "cp", "cs", "sp", "ss")}

    def tm(a):
        return jnp.transpose(a, (1, 0, 2))

    for i in range(depth):
        lw = _layer_weights(i, w_in, pool_mix, pool_scale, conv_w, conv_b, conv_ln_g, conv_ln_b, sconv_w,
                            w_br_pool, w_br_conv, w_br_sconv, w_br_attn, w_o, ln1_g, ln1_b,
                            w_ffn_in, w_ffn_out, w_pe, w_pg, ln2_g, ln2_b)
        (a, b, c, qbf, kt, kbf, vt, vtc, qi3, ki3, kwt, npool, nconf, nsc) = _mix_prompt(
            xp, N_BRANCH * dm, nb, seq, cos_p, sin_p, consts, lw)
        outs["kp"].append(jnp.transpose(kt.reshape(nb, N_HEADS, HEAD_DIM, seq), (0, 3, 1, 2)))
        outs["vp"].append(jnp.transpose(vt.reshape(nb, N_HEADS, HEAD_DIM, seq), (0, 3, 1, 2)))
        outs["kip"].append(jnp.transpose(kwt[:, :IDX_DIM, :], (0, 2, 1)))
        outs["pp"].append(npool)
        outs["cp"].append(nconf)
        outs["sp"].append(nsc)

        zs = _in_proj(xs, lw["w_in"], N_BRANCH * dm)
        zs_t = tm(zs.reshape(ns, t_new, NZ))
        (a_t, b_t, c_t, q_t, k_t, qi3_t, kw_t, npool_t, nconf_t, nsc_t) = _mix_sample(
            zs_t, tm(state_pool[i]), tm(state_conv[i]), tm(state_sconv[i]), cos_s, sin_s, consts, lw, past)
        q_n, k_n, kw_n = tm(q_t), tm(k_t), tm(kw_t)
        v_n = zs[:, COL_V:COL_V + ATT_W].reshape(ns, t_new, ATT_W)
        lhs = jnp.transpose(qi3_t.reshape(t_new, ns, IDX_HEADS, IDX_K), (1, 2, 0, 3))
        lhs = jnp.pad(lhs, ((0, 0), (0, 0), (0, SUBLANES - t_new), (0, 0))).reshape(ns, IDX_HEADS * SUBLANES, IDX_K)
        wi = jnp.transpose(kw_n[:, :, WI_LANE:WI_LANE + IDX_HEADS], (0, 2, 1))
        wi = jnp.pad(wi, ((0, 0), (0, 0), (0, SUBLANES - t_new))).reshape(ns, IDX_HEADS * SUBLANES, 1)
        wb = jnp.broadcast_to(wi, (ns, IDX_HEADS * SUBLANES, LANES))
        d, d_s = _dsa(qbf, qi3, kwt, kbf, vtc, ki3, nb, seq,
                      pt_flat, lhs, wb, _pad_rows(q_n, SUBLANES), _pad_rows(kw_n, SUBLANES),
                      _pad_rows(k_n, SUBLANES), _pad_rows(v_n, SUBLANES), cki, ck, cv,
                      i, n_phys, n_pages, t_new, consts)
        d_s = d_s[:, :t_new].reshape(ns * t_new, ATT_W)
        x1 = _merge(a, b, c, d, xp, lw, alpha)
        xp = _ffn(x1, p_prompt[i].reshape(nb * seq, -1), lw, alpha)

        def rows(a_tm):
            return tm(a_tm).reshape(ns * t_new, -1)
        x1s = _merge(rows(a_t), rows(b_t), rows(c_t), d_s, xs, lw, alpha)
        xs = _ffn(x1s, p_sample[i].reshape(ns * t_new, -1), lw, alpha)
        outs["ks"].append(k_n.reshape(ns, t_new, N_HEADS, HEAD_DIM))
        outs["vs"].append(v_n.reshape(ns, t_new, N_HEADS, HEAD_DIM))
        outs["kis"].append(kw_n[:, :, :IDX_DIM])
        outs["ps"].append(tm(npool_t))
        outs["cs"].append(tm(nconf_t))
        outs["ss"].append(tm(nsc_t))

    st = {k: jnp.stack(v) for k, v in outs.items()}
    return (xp.reshape(nb, seq, dm), xs.reshape(ns, t_new, dm),
            st["kp"], st["vp"], st["kip"], st["ks"], st["vs"], st["kis"],
            st["pp"], st["ps"], st["cp"], st["cs"], st["sp"], st["ss"])
```
